```python
import jax, jax.numpy as jnp
from jax import lax
import numpy as np

D_MODEL = 2048
BATCH = 8
SEQ = 2048
DEPTH = 1

LRU_WIDTH = 2560
LRU_HEADS = 20
LRU_HEAD_DIM = LRU_WIDTH // LRU_HEADS
LRU_CONV = 4
LRU_CONV_LEFT = 2
LRU_C = 8.0
CONV_WIDTH = 2048
CONV_GROUPS = 16
SHORT_CONV = 3
SHORT_CONV_LEFT = 1
D_FF = 4 * D_MODEL
ALPHA = float((2 * DEPTH) ** 0.25)
BETA = float((8 * DEPTH) ** -0.25)
LN_EPS = 1e-5
SPLIT_SIZES = (LRU_WIDTH, CONV_WIDTH, CONV_WIDTH, CONV_WIDTH, D_MODEL, D_MODEL)
SPLIT_POINTS = tuple(int(v) for v in np.cumsum(SPLIT_SIZES)[:-1])
IN_COLS = int(sum(SPLIT_SIZES))

kernel_name = "hybrid_rglru_shortconv_deepnorm_block"


def layer_norm(x, g, b):
    xf = x.astype(jnp.float32)
    mu = jnp.mean(xf, axis=-1, keepdims=True)
    var = jnp.mean(jnp.square(xf - mu), axis=-1, keepdims=True)
    y = (xf - mu) * lax.rsqrt(var + LN_EPS)
    return (y * g.astype(jnp.float32) + b.astype(jnp.float32)).astype(x.dtype)


def centred_dwconv(u, w, b, left):
    k_w = w.shape[0]
    s = u.shape[1]
    up = jnp.pad(u, ((0, 0), (left, k_w - 1 - left), (0, 0)))
    return sum(up[:, k:k + s] * w[k] for k in range(k_w)) + b


def _lin_combine(left, right):
    a1, b1 = left
    a2, b2 = right
    return a1 * a2, a2 * b1 + b2


def rg_lru_direction(u, w_a, b_a, w_x, b_x, lam, reverse):
    bsz, s, _ = u.shape
    uh = u.reshape(bsz, s, LRU_HEADS, LRU_HEAD_DIM)
    r = jax.nn.sigmoid(jnp.einsum('bshd,hde->bshe', uh, w_a) + b_a).reshape(bsz, s, LRU_WIDTH)
    i = jax.nn.sigmoid(jnp.einsum('bshd,hde->bshe', uh, w_x) + b_x).reshape(bsz, s, LRU_WIDTH)
    log_a = -LRU_C * r.astype(jnp.float32) * jax.nn.softplus(-lam.astype(jnp.float32))
    a = jnp.exp(log_a)
    gated_x = jnp.sqrt(-jnp.expm1(2.0 * log_a)) * (i * u).astype(jnp.float32)
    _, h = lax.associative_scan(_lin_combine, (a, gated_x), reverse=reverse, axis=1)
    return h.astype(u.dtype)


def setup_inputs(seed: int = 0) -> dict:
    key = jax.random.key(seed)
    ks = jax.random.split(key, 24)
    f32 = jnp.float32
    nrm = lambda k, shape, scale: jax.random.normal(k, shape, f32) * scale
    x = jax.random.normal(ks[0], (BATCH, SEQ, D_MODEL), f32)
    w_in = nrm(ks[1], (D_MODEL, IN_COLS), D_MODEL ** -0.5)
    lru_conv_w = nrm(ks[2], (LRU_CONV, LRU_WIDTH), LRU_CONV ** -0.5)
    lru_conv_b = nrm(ks[3], (LRU_WIDTH,), 0.01)
    lru_w_a = nrm(ks[4], (2, LRU_HEADS, LRU_HEAD_DIM, LRU_HEAD_DIM), LRU_HEAD_DIM ** -0.5)
    lru_b_a = nrm(ks[5], (2, LRU_HEADS, LRU_HEAD_DIM), 0.01)
    lru_w_x = nrm(ks[6], (2, LRU_HEADS, LRU_HEAD_DIM, LRU_HEAD_DIM), LRU_HEAD_DIM ** -0.5)
    lru_b_x = nrm(ks[7], (2, LRU_HEADS, LRU_HEAD_DIM), 0.01)
    a_c = jax.random.uniform(ks[8], (2, LRU_WIDTH), f32, 0.9, 0.999)
    a0 = a_c ** (1.0 / LRU_C)
    lru_lambda = jnp.log(a0) - jnp.log1p(-a0)
    w_lru_out = nrm(ks[9], (LRU_WIDTH, D_MODEL), BETA * LRU_WIDTH ** -0.5)
    sc_conv_w = nrm(ks[10], (SHORT_CONV, CONV_WIDTH), SHORT_CONV ** -0.5)
    sc_conv_b = nrm(ks[11], (CONV_WIDTH,), 0.01)
    w_conv_out = nrm(ks[12], (CONV_WIDTH, D_MODEL), BETA * CONV_WIDTH ** -0.5)
    w_o = nrm(ks[13], (D_MODEL, D_MODEL), BETA * D_MODEL ** -0.5)
    ln1_g = 1.0 + nrm(ks[14], (D_MODEL,), 0.01)
    ln1_b = nrm(ks[15], (D_MODEL,), 0.01)
    mlp_w1 = nrm(ks[16], (D_MODEL, D_FF), D_MODEL ** -0.5)
    mlp_b1 = nrm(ks[17], (D_FF,), 0.01)
    mlp_w2 = nrm(ks[18], (D_FF, D_MODEL), BETA * D_FF ** -0.5)
    mlp_b2 = nrm(ks[19], (D_MODEL,), 0.01)
    ln2_g = 1.0 + nrm(ks[20], (D_MODEL,), 0.01)
    ln2_b = nrm(ks[21], (D_MODEL,), 0.01)
    return {"x": x, "w_in": w_in, "lru_conv_w": lru_conv_w, "lru_conv_b": lru_conv_b,
            "lru_w_a": lru_w_a, "lru_b_a": lru_b_a, "lru_w_x": lru_w_x, "lru_b_x": lru_b_x,
            "lru_lambda": lru_lambda, "w_lru_out": w_lru_out, "sc_conv_w": sc_conv_w,
            "sc_conv_b": sc_conv_b, "w_conv_out": w_conv_out, "w_o": w_o,
            "ln1_g": ln1_g, "ln1_b": ln1_b, "mlp_w1": mlp_w1, "mlp_b1": mlp_b1,
            "mlp_w2": mlp_w2, "mlp_b2": mlp_b2, "ln2_g": ln2_g, "ln2_b": ln2_b}


def reference(x, w_in, lru_conv_w, lru_conv_b, lru_w_a, lru_b_a, lru_w_x, lru_b_x,
              lru_lambda, w_lru_out, sc_conv_w, sc_conv_b, w_conv_out, w_o,
              ln1_g, ln1_b, mlp_w1, mlp_b1, mlp_w2, mlp_b2, ln2_g, ln2_b):
    h = x
    for _ in range(DEPTH):
        proj = jnp.einsum('bsd,dc->bsc', h, w_in)
        lru_x, conv_x, conv_bg, conv_cg, g_lru, g_conv = jnp.split(proj, SPLIT_POINTS, axis=-1)
        u = centred_dwconv(lru_x, lru_conv_w, lru_conv_b, LRU_CONV_LEFT)
        h_fwd = rg_lru_direction(u, lru_w_a[0], lru_b_a[0], lru_w_x[0], lru_b_x[0], lru_lambda[0], False)
        h_bwd = rg_lru_direction(u, lru_w_a[1], lru_b_a[1], lru_w_x[1], lru_b_x[1], lru_lambda[1], True)
        y_lru = jnp.einsum('bsw,wd->bsd', h_fwd + h_bwd, w_lru_out)
        v = centred_dwconv(conv_cg * conv_x, sc_conv_w, sc_conv_b, SHORT_CONV_LEFT)
        y_conv = jnp.einsum('bsc,cd->bsd', conv_bg * v, w_conv_out)
        merged = jax.nn.sigmoid(g_lru) * y_lru + jax.nn.sigmoid(g_conv) * y_conv
        mix_out = jnp.einsum('bsd,de->bse', merged, w_o)
        h = layer_norm(ALPHA * h + mix_out, ln1_g, ln1_b)
        ff = jnp.square(jax.nn.relu(jnp.einsum('bsd,df->bsf', h, mlp_w1) + mlp_b1))
        ff_out = jnp.einsum('bsf,fd->bsd', ff, mlp_w2) + mlp_b2
        h = layer_norm(ALPHA * h + ff_out, ln2_g, ln2_b)
    return h
```

```python
import functools

import jax
import jax.numpy as jnp
from jax import lax
from jax.experimental import pallas as pl
from jax.experimental.pallas import tpu as pltpu

F32 = jnp.float32
BF16 = jnp.bfloat16

D_MODEL = 2048
BATCH = 8
SEQ = 2048
ROWS = BATCH * SEQ
LRU_WIDTH = 2560
LRU_HEADS = 20
HEAD_DIM = 128
LRU_TAPS = 4
LRU_LEFT = 2
LRU_C = 8.0
CONV_WIDTH = 2048
SC_TAPS = 3
D_FF = 4 * D_MODEL
ALPHA = float(2.0 ** 0.25)
LN_EPS = 1e-5

OFF_LRU = 0
OFF_CX = LRU_WIDTH
OFF_CB = OFF_CX + CONV_WIDTH
OFF_CC = OFF_CB + CONV_WIDTH
OFF_GL = OFF_CC + CONV_WIDTH
OFF_GC = OFF_GL + D_MODEL

VMEM_LIMIT = 56 * 1024 * 1024


def _params(sem):
    return pltpu.CompilerParams(dimension_semantics=sem, vmem_limit_bytes=VMEM_LIMIT)


def _proj_kernel(x_ref, *refs, kind):
    xb_ref = refs[-1]
    o_ref = refs[-2]

    @pl.when(pl.program_id(1) == 0)
    def _():
        xb_ref[...] = x_ref[...].astype(BF16)

    acc = jnp.dot(xb_ref[...], refs[0][...], preferred_element_type=F32)
    if kind == "prod":
        acc = acc * jnp.dot(xb_ref[...], refs[1][...], preferred_element_type=F32)
    elif kind == "sigmoid":
        acc = jax.nn.sigmoid(acc)
    o_ref[...] = acc.astype(o_ref.dtype)


def _proj(xt, w, col_offs, ncols, out_dtype, kind, tm=1024, tn=512):
    k = xt.shape[1]
    w_specs = [
        pl.BlockSpec((k, tn), functools.partial(lambda i, j, o: (0, j + o), o=off // tn))
        for off in col_offs
    ]
    return pl.pallas_call(
        functools.partial(_proj_kernel, kind=kind),
        grid=(ROWS // tm, ncols // tn),
        in_specs=[pl.BlockSpec((tm, k), lambda i, j: (i, 0))] + w_specs,
        out_specs=pl.BlockSpec((tm, tn), lambda i, j: (i, j)),
        out_shape=jax.ShapeDtypeStruct((ROWS, ncols), out_dtype),
        scratch_shapes=[pltpu.VMEM((tm, k), BF16)],
        compiler_params=_params(("parallel", "arbitrary")),
        name="proj_" + kind,
    )(xt, *([w] * len(col_offs)))


LRU_CHUNK_T = 128
LRU_CHUNK_ROWS = LRU_CHUNK_T * BATCH
LRU_NCHUNK = SEQ // LRU_CHUNK_T
LRU_PAD_TOP = LRU_LEFT * BATCH
LRU_PAD_BOT = (LRU_TAPS - 1 - LRU_LEFT) * BATCH


def _softplus(x):
    return jnp.maximum(x, 0.0) + jnp.log1p(jnp.exp(-jnp.abs(x)))


def _lru_kernel(lx_ref, cw_ref, cb_ref, wg_ref, bg_ref, lam_ref, o_ref,
                xp_ref, acc_ref, af_ref, bf_ref, ab_ref, bb_ref):
    rc = LRU_CHUNK_ROWS
    xp_ref[pl.ds(0, LRU_PAD_TOP), :] = jnp.zeros((LRU_PAD_TOP, HEAD_DIM), F32)
    xp_ref[pl.ds(LRU_PAD_TOP, ROWS), :] = lx_ref[...]
    xp_ref[pl.ds(LRU_PAD_TOP + ROWS, LRU_PAD_BOT), :] = jnp.zeros((LRU_PAD_BOT, HEAD_DIM), F32)
    acc_ref[...] = jnp.zeros_like(acc_ref)

    decay = -LRU_C * _softplus(-lam_ref[0])
    cw = cw_ref[...]
    cb = cb_ref[...]
    wg = wg_ref[0]
    bg = bg_ref[0]

    def gates(r0, d):
        u = cb + cw[0:1] * xp_ref[pl.ds(r0, rc), :]
        for k in range(1, LRU_TAPS):
            u = u + cw[k:k + 1] * xp_ref[pl.ds(r0 + k * BATCH, rc), :]
        lo = 2 * HEAD_DIM * d
        g = jnp.dot(u.astype(BF16), wg[:, lo:lo + 2 * HEAD_DIM],
                    preferred_element_type=F32) + bg[:, lo:lo + 2 * HEAD_DIM]
        r = jax.nn.sigmoid(g[:, :HEAD_DIM])
        i = jax.nn.sigmoid(g[:, HEAD_DIM:])
        log_a = decay[d:d + 1] * r
        a = jnp.exp(log_a)
        b = jnp.sqrt(-jnp.tanh(log_a) * (1.0 + a * a)) * (i * u)
        return a, b

    def chunk(c, carry):
        rf = pl.multiple_of(c * rc, rc)
        rb = pl.multiple_of((LRU_NCHUNK - 1 - c) * rc, rc)
        a, b = gates(rf, 0)
        af_ref[...] = a
        bf_ref[...] = b
        a, b = gates(rb, 1)
        ab_ref[...] = a
        bb_ref[...] = b

        def step(t, hc):
            hf, hb = hc
            of = pl.multiple_of(t * BATCH, BATCH)
            ob = pl.multiple_of((LRU_CHUNK_T - 1 - t) * BATCH, BATCH)
            hf = af_ref[pl.ds(of, BATCH), :] * hf + bf_ref[pl.ds(of, BATCH), :]
            hb = ab_ref[pl.ds(ob, BATCH), :] * hb + bb_ref[pl.ds(ob, BATCH), :]
            acc_ref[pl.ds(rf + of, BATCH), :] += hf
            acc_ref[pl.ds(rb + ob, BATCH), :] += hb
            return hf, hb

        return lax.fori_loop(0, LRU_CHUNK_T, step, carry, unroll=8)

    h0 = jnp.zeros((BATCH, HEAD_DIM), F32)
    lax.fori_loop(0, LRU_NCHUNK, chunk, (h0, h0))
    o_ref[...] = acc_ref[...].astype(o_ref.dtype)


def _lru(lx, cw, cb, wg, bg, lam):
    rc = LRU_CHUNK_ROWS
    return pl.pallas_call(
        _lru_kernel,
        grid=(LRU_HEADS,),
        in_specs=[
            pl.BlockSpec((ROWS, HEAD_DIM), lambda h: (0, h)),
            pl.BlockSpec((LRU_TAPS, HEAD_DIM), lambda h: (0, h)),
            pl.BlockSpec((1, HEAD_DIM), lambda h: (0, h)),
            pl.BlockSpec((1, HEAD_DIM, 4 * HEAD_DIM), lambda h: (h, 0, 0)),
            pl.BlockSpec((1, 1, 4 * HEAD_DIM), lambda h: (h, 0, 0)),
            pl.BlockSpec((1, 2, HEAD_DIM), lambda h: (h, 0, 0)),
        ],
        out_specs=pl.BlockSpec((ROWS, HEAD_DIM), lambda h: (0, h)),
        out_shape=jax.ShapeDtypeStruct((ROWS, LRU_WIDTH), BF16),
        scratch_shapes=[
            pltpu.VMEM((LRU_PAD_TOP + ROWS + LRU_PAD_BOT, HEAD_DIM), F32),
            pltpu.VMEM((ROWS, HEAD_DIM), F32),
            pltpu.VMEM((rc, HEAD_DIM), F32),
            pltpu.VMEM((rc, HEAD_DIM), F32),
            pltpu.VMEM((rc, HEAD_DIM), F32),
            pltpu.VMEM((rc, HEAD_DIM), F32),
        ],
        compiler_params=_params(("parallel",)),
        name="lru_mixer",
    )(lx, cw, cb, wg, bg, lam)


def _merge_kernel(w_ref, wprev_ref, wnext_ref, bg_ref, hl_ref, sgl_ref, sgc_ref,
                  scw_ref, scb_ref, wlo_ref, wco_ref, o_ref, z_ref):
    i = pl.program_id(0)

    @pl.when(pl.program_id(1) == 0)
    def _():
        tm = w_ref.shape[0]
        wc = w_ref[...]
        prev = jnp.where(i == 0, 0.0, wprev_ref[...])
        nxt = jnp.where(i == pl.num_programs(0) - 1, 0.0, wnext_ref[...])
        w_dn = jnp.concatenate([prev, wc[:tm - BATCH]], axis=0)
        w_up = jnp.concatenate([wc[BATCH:], nxt], axis=0)
        scw = scw_ref[...]
        v = scw[0:1] * w_dn + scw[1:2] * wc + scw[2:3] * w_up + scb_ref[...]
        z_ref[...] = (bg_ref[...].astype(F32) * v).astype(BF16)

    y_lru = jnp.dot(hl_ref[...], wlo_ref[...], preferred_element_type=F32)
    y_conv = jnp.dot(z_ref[...], wco_ref[...], preferred_element_type=F32)
    merged = sgl_ref[...].astype(F32) * y_lru + sgc_ref[...].astype(F32) * y_conv
    o_ref[...] = merged.astype(o_ref.dtype)


def _merge(w, bgate, hl, sg, scw, scb, wlo, wco, tm=512, tn=512):
    nb = tm // BATCH
    last = ROWS // BATCH - 1
    gc_off = D_MODEL // tn
    return pl.pallas_call(
        _merge_kernel,
        grid=(ROWS // tm, D_MODEL // tn),
        in_specs=[
            pl.BlockSpec((tm, CONV_WIDTH), lambda i, j: (i, 0)),
            pl.BlockSpec((BATCH, CONV_WIDTH), lambda i, j: (jnp.maximum(i * nb - 1, 0), 0)),
            pl.BlockSpec((BATCH, CONV_WIDTH), lambda i, j: (jnp.minimum((i + 1) * nb, last), 0)),
            pl.BlockSpec((tm, CONV_WIDTH), lambda i, j: (i, 0)),
            pl.BlockSpec((tm, LRU_WIDTH), lambda i, j: (i, 0)),
            pl.BlockSpec((tm, tn), lambda i, j: (i, j)),
            pl.BlockSpec((tm, tn), lambda i, j: (i, j + gc_off)),
            pl.BlockSpec((SC_TAPS, CONV_WIDTH), lambda i, j: (0, 0)),
            pl.BlockSpec((1, CONV_WIDTH), lambda i, j: (0, 0)),
            pl.BlockSpec((LRU_WIDTH, tn), lambda i, j: (0, j)),
            pl.BlockSpec((CONV_WIDTH, tn), lambda i, j: (0, j)),
        ],
        out_specs=pl.BlockSpec((tm, tn), lambda i, j: (i, j)),
        out_shape=jax.ShapeDtypeStruct((ROWS, D_MODEL), BF16),
        scratch_shapes=[pltpu.VMEM((tm, CONV_WIDTH), BF16)],
        compiler_params=_params(("parallel", "arbitrary")),
        name="mixer_merge",
    )(w, w, w, bgate, hl, sg, sg, scw, scb, wlo, wco)


def _layer_norm(y, g, b):
    mu = jnp.mean(y, axis=-1, keepdims=True)
    d = y - mu
    var = jnp.mean(d * d, axis=-1, keepdims=True)
    return d * lax.rsqrt(var + LN_EPS) * g + b


def _outproj_kernel(m_ref, x_ref, wo_ref, g_ref, b_ref, o_ref):
    y = jnp.dot(m_ref[...], wo_ref[...], preferred_element_type=F32)
    o_ref[...] = _layer_norm(ALPHA * x_ref[...] + y, g_ref[...], b_ref[...])


def _outproj(merged, xt, wo, g, b, tm=512):
    return pl.pallas_call(
        _outproj_kernel,
        grid=(ROWS // tm,),
        in_specs=[
            pl.BlockSpec((tm, D_MODEL), lambda i: (i, 0)),
            pl.BlockSpec((tm, D_MODEL), lambda i: (i, 0)),
            pl.BlockSpec((D_MODEL, D_MODEL), lambda i: (0, 0)),
            pl.BlockSpec((1, D_MODEL), lambda i: (0, 0)),
            pl.BlockSpec((1, D_MODEL), lambda i: (0, 0)),
        ],
        out_specs=pl.BlockSpec((tm, D_MODEL), lambda i: (i, 0)),
        out_shape=jax.ShapeDtypeStruct((ROWS, D_MODEL), F32),
        compiler_params=_params(("parallel",)),
        name="out_proj_ln1",
    )(merged, xt, wo, g, b)


def _mlp_kernel(h_ref, w1_ref, b1_ref, w2_ref, b2_ref, g_ref, b_ref, o_ref, hb_ref, acc_ref):
    f = pl.program_id(1)

    @pl.when(f == 0)
    def _():
        hb_ref[...] = h_ref[...].astype(BF16)
        acc_ref[...] = jnp.zeros_like(acc_ref)

    a = jnp.dot(hb_ref[...], w1_ref[...], preferred_element_type=F32) + b1_ref[...]
    a = jnp.maximum(a, 0.0)
    acc_ref[...] += jnp.dot((a * a).astype(BF16), w2_ref[...], preferred_element_type=F32)

    @pl.when(f == pl.num_programs(1) - 1)
    def _():
        y = ALPHA * h_ref[...] + acc_ref[...] + b2_ref[...]
        o_ref[...] = _layer_norm(y, g_ref[...], b_ref[...])


def _mlp(h, w1, b1, w2, b2, g, b, tm=512, tf=512):
    return pl.pallas_call(
        _mlp_kernel,
        grid=(ROWS // tm, D_FF // tf),
        in_specs=[
            pl.BlockSpec((tm, D_MODEL), lambda i, f: (i, 0)),
            pl.BlockSpec((D_MODEL, tf), lambda i, f: (0, f)),
            pl.BlockSpec((1, tf), lambda i, f: (0, f)),
            pl.BlockSpec((tf, D_MODEL), lambda i, f: (f, 0)),
            pl.BlockSpec((1, D_MODEL), lambda i, f: (0, 0)),
            pl.BlockSpec((1, D_MODEL), lambda i, f: (0, 0)),
            pl.BlockSpec((1, D_MODEL), lambda i, f: (0, 0)),
        ],
        out_specs=pl.BlockSpec((tm, D_MODEL), lambda i, f: (i, 0)),
        out_shape=jax.ShapeDtypeStruct((ROWS, D_MODEL), F32),
        scratch_shapes=[pltpu.VMEM((tm, D_MODEL), BF16), pltpu.VMEM((tm, D_MODEL), F32)],
        compiler_params=_params(("parallel", "arbitrary")),
        name="mlp_ln2",
    )(h, w1, b1, w2, b2, g, b)


def kernel(x, w_in, lru_conv_w, lru_conv_b, lru_w_a, lru_b_a, lru_w_x, lru_b_x, lru_lambda,
           w_lru_out, sc_conv_w, sc_conv_b, w_conv_out, w_o, ln1_g, ln1_b, mlp_w1, mlp_b1,
           mlp_w2, mlp_b2, ln2_g, ln2_b):
    xt = jnp.transpose(x, (1, 0, 2)).reshape(ROWS, D_MODEL)
    w_in_b = w_in.astype(BF16)

    lx = _proj(xt, w_in_b, (OFF_LRU,), LRU_WIDTH, F32, "plain")
    cxc = _proj(xt, w_in_b, (OFF_CX, OFF_CC), CONV_WIDTH, F32, "prod")
    cbg = _proj(xt, w_in_b, (OFF_CB,), CONV_WIDTH, BF16, "plain")
    sg = _proj(xt, w_in_b, (OFF_GL,), 2 * D_MODEL, BF16, "sigmoid")

    wg = jnp.concatenate([lru_w_a[0], lru_w_x[0], lru_w_a[1], lru_w_x[1]], axis=-1).astype(BF16)
    bg = jnp.concatenate([lru_b_a[0], lru_b_x[0], lru_b_a[1], lru_b_x[1]], axis=-1)
    bg = bg.reshape(LRU_HEADS, 1, 4 * HEAD_DIM)
    lam = lru_lambda.reshape(2, LRU_HEADS, HEAD_DIM).transpose(1, 0, 2)
    hl = _lru(lx, lru_conv_w, lru_conv_b.reshape(1, LRU_WIDTH), wg, bg, lam)

    merged = _merge(cxc, cbg, hl, sg, sc_conv_w, sc_conv_b.reshape(1, CONV_WIDTH),
                    w_lru_out.astype(BF16), w_conv_out.astype(BF16))
    h1 = _outproj(merged, xt, w_o.astype(BF16), ln1_g.reshape(1, D_MODEL), ln1_b.reshape(1, D_MODEL))
    h2 = _mlp(h1, mlp_w1.astype(BF16), mlp_b1.reshape(1, D_FF), mlp_w2.astype(BF16),
              mlp_b2.reshape(1, D_MODEL), ln2_g.reshape(1, D_MODEL), ln2_b.reshape(1, D_MODEL))
    return jnp.transpose(h2.reshape(SEQ, BATCH, D_MODEL), (1, 0, 2))
```

```python
import functools

import jax
import jax.numpy as jnp
from jax import lax
from jax.experimental import pallas as pl
from jax.experimental.pallas import tpu as pltpu

F32 = jnp.float32
BF16 = jnp.bfloat16

D_MODEL = 2048
BATCH = 8
SEQ = 2048
ROWS = BATCH * SEQ
LRU_WIDTH = 2560
LRU_HEADS = 20
HEAD_DIM = 128
LRU_TAPS = 4
LRU_LEFT = 2
LRU_C = 8.0
CONV_WIDTH = 2048
SC_TAPS = 3
D_FF = 4 * D_MODEL
ALPHA = float(2.0 ** 0.25)
LN_EPS = 1e-5
LANES = 128

OFF_LRU = 0
OFF_CX = LRU_WIDTH
OFF_CB = OFF_CX + CONV_WIDTH
OFF_CC = OFF_CB + CONV_WIDTH
OFF_GL = OFF_CC + CONV_WIDTH
OFF_GC = OFF_GL + D_MODEL

VMEM_LIMIT = 56 * 1024 * 1024


def _params(sem):
    return pltpu.CompilerParams(dimension_semantics=sem, vmem_limit_bytes=VMEM_LIMIT)


def _proj_kernel(x_ref, *refs, kind):
    o_ref = refs[-1]
    acc = jnp.dot(x_ref[...], refs[0][...], preferred_element_type=F32)
    if kind == "prod":
        acc = acc * jnp.dot(x_ref[...], refs[1][...], preferred_element_type=F32)
    elif kind == "sigmoid":
        acc = jax.nn.sigmoid(acc)
    o_ref[...] = acc.astype(o_ref.dtype)


def _proj(xt, w, col_offs, ncols, out_dtype, kind, tm=2048, tn=512):
    k = xt.shape[1]
    w_specs = [
        pl.BlockSpec((k, tn), functools.partial(lambda i, j, o: (0, j + o), o=off // tn))
        for off in col_offs
    ]
    return pl.pallas_call(
        functools.partial(_proj_kernel, kind=kind),
        grid=(ROWS // tm, ncols // tn),
        in_specs=[pl.BlockSpec((tm, k), lambda i, j: (i, 0))] + w_specs,
        out_specs=pl.BlockSpec((tm, tn), lambda i, j: (i, j)),
        out_shape=jax.ShapeDtypeStruct((ROWS, ncols), out_dtype),
        compiler_params=_params(("parallel", "arbitrary")),
        name="proj_" + kind,
    )(xt, *([w] * len(col_offs)))


LRU_CHUNK_T = 128
LRU_CHUNK_ROWS = LRU_CHUNK_T * BATCH
LRU_NCHUNK = SEQ // LRU_CHUNK_T


def _softplus(x):
    return jnp.maximum(x, 0.0) + jnp.log1p(jnp.exp(-jnp.abs(x)))


def _lru_kernel(lx_ref, cw_ref, cb_ref, wg_ref, bg_ref, lam_ref, o_ref,
                u_ref, acc_ref, af_ref, bf_ref, ab_ref, bb_ref, hf_ref, hb_ref):
    rc = LRU_CHUNK_ROWS
    cw = cw_ref[...]
    cb = cb_ref[...]

    def conv_chunk(r0, first, last):
        u = cb
        for k in range(LRU_TAPS):
            off = (k - LRU_LEFT) * BATCH
            if first and off < 0:
                tap = jnp.concatenate(
                    [jnp.zeros((-off, HEAD_DIM), F32), lx_ref[pl.ds(0, rc + off), :]], axis=0)
            elif last and off > 0:
                tap = jnp.concatenate(
                    [lx_ref[pl.ds(r0 + off, rc - off), :], jnp.zeros((off, HEAD_DIM), F32)], axis=0)
            else:
                tap = lx_ref[pl.ds(r0 + off, rc), :]
            u = u + cw[k:k + 1] * tap
        u_ref[pl.ds(r0, rc), :] = u

    conv_chunk(0, True, False)

    def conv_body(c, carry):
        conv_chunk(pl.multiple_of(c * rc, rc), False, False)
        return carry

    lax.fori_loop(1, LRU_NCHUNK - 1, conv_body, 0)
    conv_chunk(ROWS - rc, False, True)

    half_decay = (-0.5 * LRU_C) * _softplus(-lam_ref[0])
    wg = wg_ref[0]
    bg = bg_ref[0]

    def gates(r0, d):
        u = u_ref[pl.ds(r0, rc), :]
        lo = 2 * HEAD_DIM * d
        g = jnp.dot(u.astype(BF16), wg[:, lo:lo + 2 * HEAD_DIM],
                    preferred_element_type=F32) + bg[:, lo:lo + 2 * HEAD_DIM]
        tr = jnp.tanh(0.5 * g[:, :HEAD_DIM])
        ti = jnp.tanh(0.5 * g[:, HEAD_DIM:])
        hd = half_decay[d:d + 1]
        log_a = hd * tr + hd
        a = jnp.exp(log_a)
        b = jnp.sqrt(jnp.tanh(log_a) * (-1.0 - a * a)) * ((0.5 * ti + 0.5) * u)
        return a, b

    def scan_chunk(c, carry, second_half):
        rf = pl.multiple_of(c * rc, rc)
        rb = pl.multiple_of((LRU_NCHUNK - 1 - c) * rc, rc)
        a, b = gates(rf, 0)
        af_ref[...] = a
        bf_ref[...] = b
        a, b = gates(rb, 1)
        ab_ref[...] = a
        bb_ref[...] = b

        def step(t, hc):
            hf, hb = hc
            of = pl.multiple_of(t * BATCH, BATCH)
            ob = pl.multiple_of((LRU_CHUNK_T - 1 - t) * BATCH, BATCH)
            hf = af_ref[pl.ds(of, BATCH), :] * hf + bf_ref[pl.ds(of, BATCH), :]
            hb = ab_ref[pl.ds(ob, BATCH), :] * hb + bb_ref[pl.ds(ob, BATCH), :]
            hf_ref[pl.ds(of, BATCH), :] = hf
            hb_ref[pl.ds(ob, BATCH), :] = hb
            return hf, hb

        carry = lax.fori_loop(0, LRU_CHUNK_T, step, carry, unroll=8)
        if second_half:
            o_ref[pl.ds(rf, rc), :] = (acc_ref[pl.ds(rf, rc), :] + hf_ref[...]).astype(o_ref.dtype)
            o_ref[pl.ds(rb, rc), :] = (acc_ref[pl.ds(rb, rc), :] + hb_ref[...]).astype(o_ref.dtype)
        else:
            acc_ref[pl.ds(rf, rc), :] = hf_ref[...]
            acc_ref[pl.ds(rb, rc), :] = hb_ref[...]
        return carry

    h0 = jnp.zeros((BATCH, HEAD_DIM), F32)
    carry = lax.fori_loop(0, LRU_NCHUNK // 2,
                          functools.partial(scan_chunk, second_half=False), (h0, h0))
    lax.fori_loop(LRU_NCHUNK // 2, LRU_NCHUNK,
                  functools.partial(scan_chunk, second_half=True), carry)


def _lru(lx, cw, cb, wg, bg, lam):
    chunk = pltpu.VMEM((LRU_CHUNK_ROWS, HEAD_DIM), F32)
    return pl.pallas_call(
        _lru_kernel,
        grid=(LRU_HEADS,),
        in_specs=[
            pl.BlockSpec((ROWS, HEAD_DIM), lambda h: (0, h)),
            pl.BlockSpec((LRU_TAPS, HEAD_DIM), lambda h: (0, h)),
            pl.BlockSpec((1, HEAD_DIM), lambda h: (0, h)),
            pl.BlockSpec((1, HEAD_DIM, 4 * HEAD_DIM), lambda h: (h, 0, 0)),
            pl.BlockSpec((1, 1, 4 * HEAD_DIM), lambda h: (h, 0, 0)),
            pl.BlockSpec((1, 2, HEAD_DIM), lambda h: (h, 0, 0)),
        ],
        out_specs=pl.BlockSpec((ROWS, HEAD_DIM), lambda h: (0, h)),
        out_shape=jax.ShapeDtypeStruct((ROWS, LRU_WIDTH), BF16),
        scratch_shapes=[
            pltpu.VMEM((ROWS, HEAD_DIM), F32),
            pltpu.VMEM((ROWS, HEAD_DIM), F32),
            chunk, chunk, chunk, chunk, chunk, chunk,
        ],
        compiler_params=_params(("parallel",)),
        name="lru_mixer",
    )(lx, cw, cb, wg, bg, lam)


def _merge_kernel(w_ref, wprev_ref, wnext_ref, bg_ref, hl_ref, sgl_ref, sgc_ref,
                  scw_ref, scb_ref, wlo_ref, wco_ref, o_ref, z_ref, res_ref):
    i = pl.program_id(0)
    tm = w_ref.shape[0]

    @pl.when(pl.program_id(1) == 0)
    def _():
        wc = w_ref[...]
        prev = jnp.where(i == 0, 0.0, wprev_ref[...])
        nxt = jnp.where(i == pl.num_programs(0) - 1, 0.0, wnext_ref[...])
        w_dn = jnp.concatenate([prev, wc[:tm - BATCH]], axis=0)
        w_up = jnp.concatenate([wc[BATCH:], nxt], axis=0)
        scw = scw_ref[...]
        v = scw[0:1] * w_dn + scw[1:2] * wc + scw[2:3] * w_up + scb_ref[...]
        z_ref[...] = (bg_ref[...].astype(F32) * v).astype(BF16)

    y_lru = jnp.dot(hl_ref[...], wlo_ref[...], preferred_element_type=F32)
    y_conv = jnp.dot(z_ref[...], wco_ref[...], preferred_element_type=F32)
    merged = sgl_ref[...].astype(F32) * y_lru + sgc_ref[...].astype(F32) * y_conv

    nslab = res_ref.shape[0]
    for k in range(nslab):
        res_ref[k] = merged[:, k * LANES:(k + 1) * LANES]
    for b in range(BATCH):
        for k in range(nslab):
            o_ref[b, :, k * LANES:(k + 1) * LANES] = (
                res_ref[k, pl.ds(b, tm // BATCH, stride=BATCH), :].astype(o_ref.dtype))


def _merge(w, bgate, hl, sg, scw, scb, wlo, wco, tm=512, tn=512):
    nb = tm // BATCH
    last = ROWS // BATCH - 1
    gc_off = D_MODEL // tn
    return pl.pallas_call(
        _merge_kernel,
        grid=(ROWS // tm, D_MODEL // tn),
        in_specs=[
            pl.BlockSpec((tm, CONV_WIDTH), lambda i, j: (i, 0)),
            pl.BlockSpec((BATCH, CONV_WIDTH), lambda i, j: (jnp.maximum(i * nb - 1, 0), 0)),
            pl.BlockSpec((BATCH, CONV_WIDTH), lambda i, j: (jnp.minimum((i + 1) * nb, last), 0)),
            pl.BlockSpec((tm, CONV_WIDTH), lambda i, j: (i, 0)),
            pl.BlockSpec((tm, LRU_WIDTH), lambda i, j: (i, 0)),
            pl.BlockSpec((tm, tn), lambda i, j: (i, j)),
            pl.BlockSpec((tm, tn), lambda i, j: (i, j + gc_off)),
            pl.BlockSpec((SC_TAPS, CONV_WIDTH), lambda i, j: (0, 0)),
            pl.BlockSpec((1, CONV_WIDTH), lambda i, j: (0, 0)),
            pl.BlockSpec((LRU_WIDTH, tn), lambda i, j: (0, j)),
            pl.BlockSpec((CONV_WIDTH, tn), lambda i, j: (0, j)),
        ],
        out_specs=pl.BlockSpec((BATCH, tm // BATCH, tn), lambda i, j: (0, i, j)),
        out_shape=jax.ShapeDtypeStruct((BATCH, SEQ, D_MODEL), BF16),
        scratch_shapes=[pltpu.VMEM((tm, CONV_WIDTH), BF16),
                        pltpu.VMEM((tn // LANES, tm, LANES), F32)],
        compiler_params=_params(("parallel", "arbitrary")),
        name="mixer_merge",
    )(w, w, w, bgate, hl, sg, sg, scw, scb, wlo, wco)


def _layer_norm(y, g, b):
    mu = jnp.mean(y, axis=-1, keepdims=True)
    d = y - mu
    var = jnp.mean(d * d, axis=-1, keepdims=True)
    return d * lax.rsqrt(var + LN_EPS) * g + b


def _outproj_kernel(m_ref, x_ref, wo_ref, g_ref, b_ref, o_ref):
    y = jnp.dot(m_ref[...], wo_ref[...], preferred_element_type=F32)
    o_ref[...] = _layer_norm(ALPHA * x_ref[...] + y, g_ref[...], b_ref[...])


def _outproj(merged, x2d, wo, g, b, tm=512):
    return pl.pallas_call(
        _outproj_kernel,
        grid=(ROWS // tm,),
        in_specs=[
            pl.BlockSpec((tm, D_MODEL), lambda i: (i, 0)),
            pl.BlockSpec((tm, D_MODEL), lambda i: (i, 0)),
            pl.BlockSpec((D_MODEL, D_MODEL), lambda i: (0, 0)),
            pl.BlockSpec((1, D_MODEL), lambda i: (0, 0)),
            pl.BlockSpec((1, D_MODEL), lambda i: (0, 0)),
        ],
        out_specs=pl.BlockSpec((tm, D_MODEL), lambda i: (i, 0)),
        out_shape=jax.ShapeDtypeStruct((ROWS, D_MODEL), F32),
        compiler_params=_params(("parallel",)),
        name="out_proj_ln1",
    )(merged, x2d, wo, g, b)


def _mlp_kernel(h_ref, w1_ref, b1_ref, w2_ref, b2_ref, g_ref, b_ref, o_ref, hb_ref):
    f = pl.program_id(1)

    @pl.when(f == 0)
    def _():
        h = h_ref[...]
        hb_ref[...] = h.astype(BF16)
        o_ref[...] = ALPHA * h + b2_ref[...]

    a = jnp.dot(hb_ref[...], w1_ref[...], preferred_element_type=F32) + b1_ref[...]
    a = jnp.maximum(a, 0.0)
    o_ref[...] += jnp.dot((a * a).astype(BF16), w2_ref[...], preferred_element_type=F32)

    @pl.when(f == pl.num_programs(1) - 1)
    def _():
        o_ref[...] = _layer_norm(o_ref[...], g_ref[...], b_ref[...])


def _mlp(h, w1, b1, w2, b2, g, b, tm=1024, tf=512):
    return pl.pallas_call(
        _mlp_kernel,
        grid=(ROWS // tm, D_FF // tf),
        in_specs=[
            pl.BlockSpec((tm, D_MODEL), lambda i, f: (i, 0)),
            pl.BlockSpec((D_MODEL, tf), lambda i, f: (0, f)),
            pl.BlockSpec((1, tf), lambda i, f: (0, f)),
            pl.BlockSpec((tf, D_MODEL), lambda i, f: (f, 0)),
            pl.BlockSpec((1, D_MODEL), lambda i, f: (0, 0)),
            pl.BlockSpec((1, D_MODEL), lambda i, f: (0, 0)),
            pl.BlockSpec((1, D_MODEL), lambda i, f: (0, 0)),
        ],
        out_specs=pl.BlockSpec((tm, D_MODEL), lambda i, f: (i, 0)),
        out_shape=jax.ShapeDtypeStruct((ROWS, D_MODEL), F32),
        scratch_shapes=[pltpu.VMEM((tm, D_MODEL), BF16)],
        compiler_params=_params(("parallel", "arbitrary")),
        name="mlp_ln2",
    )(h, w1, b1, w2, b2, g, b)


def kernel(x, w_in, lru_conv_w, lru_conv_b, lru_w_a, lru_b_a, lru_w_x, lru_b_x, lru_lambda,
           w_lru_out, sc_conv_w, sc_conv_b, w_conv_out, w_o, ln1_g, ln1_b, mlp_w1, mlp_b1,
           mlp_w2, mlp_b2, ln2_g, ln2_b):
    xt = jnp.transpose(x, (1, 0, 2)).reshape(ROWS, D_MODEL).astype(BF16)
    w_in_b = w_in.astype(BF16)

    lx = _proj(xt, w_in_b, (OFF_LRU,), LRU_WIDTH, F32, "plain")
    cxc = _proj(xt, w_in_b, (OFF_CX, OFF_CC), CONV_WIDTH, F32, "prod")
    cbg = _proj(xt, w_in_b, (OFF_CB,), CONV_WIDTH, BF16, "plain")
    sg = _proj(xt, w_in_b, (OFF_GL,), 2 * D_MODEL, BF16, "sigmoid")

    wg = jnp.concatenate([lru_w_a[0], lru_w_x[0], lru_w_a[1], lru_w_x[1]], axis=-1).astype(BF16)
    bg = jnp.concatenate([lru_b_a[0], lru_b_x[0], lru_b_a[1], lru_b_x[1]], axis=-1)
    bg = bg.reshape(LRU_HEADS, 1, 4 * HEAD_DIM)
    lam = lru_lambda.reshape(2, LRU_HEADS, HEAD_DIM).transpose(1, 0, 2)
    hl = _lru(lx, lru_conv_w, lru_conv_b.reshape(1, LRU_WIDTH), wg, bg, lam)

    merged = _merge(cxc, cbg, hl, sg, sc_conv_w, sc_conv_b.reshape(1, CONV_WIDTH),
                    w_lru_out.astype(BF16), w_conv_out.astype(BF16))
    h1 = _outproj(merged.reshape(ROWS, D_MODEL), x.reshape(ROWS, D_MODEL), w_o.astype(BF16),
                  ln1_g.reshape(1, D_MODEL), ln1_b.reshape(1, D_MODEL))
    h2 = _mlp(h1, mlp_w1.astype(BF16), mlp_b1.reshape(1, D_FF), mlp_w2.astype(BF16),
              mlp_b2.reshape(1, D_MODEL), ln2_g.reshape(1, D_MODEL), ln2_b.reshape(1, D_MODEL))
    return h2.reshape(BATCH, SEQ, D_MODEL)
```

```python
import functools

import jax
import jax.numpy as jnp
from jax import lax
from jax.experimental import pallas as pl
from jax.experimental.pallas import tpu as pltpu

F32 = jnp.float32
BF16 = jnp.bfloat16

D_MODEL = 2048
BATCH = 8
SEQ = 2048
ROWS = BATCH * SEQ
LRU_WIDTH = 2560
LRU_HEADS = 20
HEAD_DIM = 128
LRU_TAPS = 4
LRU_LEFT = 2
LRU_C = 8.0
CONV_WIDTH = 2048
SC_TAPS = 3
D_FF = 4 * D_MODEL
ALPHA = float(2.0 ** 0.25)
LN_EPS = 1e-5
LANES = 128

OFF_LRU = 0
OFF_CX = LRU_WIDTH
OFF_CB = OFF_CX + CONV_WIDTH
OFF_CC = OFF_CB + CONV_WIDTH
OFF_GL = OFF_CC + CONV_WIDTH
IN_COLS = OFF_GL + 2 * D_MODEL

VMEM_LIMIT = 56 * 1024 * 1024


def _params(sem):
    return pltpu.CompilerParams(dimension_semantics=sem, vmem_limit_bytes=VMEM_LIMIT)


def _proj_kernel(x_ref, *refs, kind):
    o_ref = refs[-1]
    acc = jnp.dot(x_ref[...], refs[0][...], preferred_element_type=F32)
    if kind == "prod":
        acc = acc * jnp.dot(x_ref[...], refs[1][...], preferred_element_type=F32)
    elif kind == "sigmoid":
        acc = jax.nn.sigmoid(acc)
    if kind == "heads":
        for h in range(o_ref.shape[0]):
            o_ref[h] = acc[:, h * HEAD_DIM:(h + 1) * HEAD_DIM]
    else:
        o_ref[...] = acc.astype(o_ref.dtype)


def _proj(xt, ws, out_dtype, kind, tm, tn):
    k, ncols = ws[0].shape
    assert ncols % tn == 0 and ROWS % tm == 0
    w_specs = [pl.BlockSpec((k, tn), lambda i, j: (0, j)) for _ in ws]
    if kind == "heads":
        nh = tn // HEAD_DIM
        out_spec = pl.BlockSpec((nh, tm, HEAD_DIM), lambda i, j: (j, i, 0))
        out_shape = jax.ShapeDtypeStruct((ncols // HEAD_DIM, ROWS, HEAD_DIM), out_dtype)
    else:
        out_spec = pl.BlockSpec((tm, tn), lambda i, j: (i, j))
        out_shape = jax.ShapeDtypeStruct((ROWS, ncols), out_dtype)
    return pl.pallas_call(
        functools.partial(_proj_kernel, kind=kind),
        grid=(ROWS // tm, ncols // tn),
        in_specs=[pl.BlockSpec((tm, k), lambda i, j: (i, 0))] + w_specs,
        out_specs=out_spec,
        out_shape=out_shape,
        compiler_params=_params(("parallel", "arbitrary")),
        name="proj_" + kind,
    )(xt, *ws)


LRU_CHUNK_T = 128
LRU_CHUNK_ROWS = LRU_CHUNK_T * BATCH
LRU_NCHUNK = SEQ // LRU_CHUNK_T


def _softplus(x):
    return jnp.maximum(x, 0.0) + jnp.log1p(jnp.exp(-jnp.abs(x)))


def _lru_kernel(lx_ref, cw_ref, cb_ref, wg_ref, bg_ref, lam_ref, o_ref, u_ref, acc_ref, *chunk_refs):
    bufs0, bufs1 = chunk_refs[:6], chunk_refs[6:]
    rc = LRU_CHUNK_ROWS
    nc = LRU_NCHUNK
    cw = cw_ref[...]
    cb = cb_ref[...]

    def conv_chunk(r0, first, last):
        u = cb
        for k in range(LRU_TAPS):
            off = (k - LRU_LEFT) * BATCH
            if first and off < 0:
                tap = jnp.concatenate(
                    [jnp.zeros((-off, HEAD_DIM), F32), lx_ref[pl.ds(0, rc + off), :]], axis=0)
            elif last and off > 0:
                tap = jnp.concatenate(
                    [lx_ref[pl.ds(r0 + off, rc - off), :], jnp.zeros((off, HEAD_DIM), F32)], axis=0)
            else:
                tap = lx_ref[pl.ds(r0 + off, rc), :]
            u = u + cw[k:k + 1] * tap
        u_ref[pl.ds(r0, rc), :] = u

    conv_chunk(0, True, False)

    def conv_body(c, carry):
        conv_chunk(pl.multiple_of(c * rc, rc), False, False)
        return carry

    lax.fori_loop(1, nc - 1, conv_body, 0)
    conv_chunk(ROWS - rc, False, True)

    half_decay = (-0.5 * LRU_C) * _softplus(-lam_ref[0])
    wg = wg_ref[0]
    bg = bg_ref[0]

    def chunk_row(c):
        return c * rc if isinstance(c, int) else pl.multiple_of(c * rc, rc)

    def gates(c, bufs):
        af_ref, bf_ref, ab_ref, bb_ref = bufs[:4]
        for d, (a_ref, b_ref) in enumerate(((af_ref, bf_ref), (ab_ref, bb_ref))):
            r0 = chunk_row(c if d == 0 else nc - 1 - c)
            uh = u_ref[pl.ds(r0, rc), :]
            lo = 2 * HEAD_DIM * d
            g = jnp.dot(uh.astype(BF16), wg[:, lo:lo + 2 * HEAD_DIM],
                        preferred_element_type=F32) + bg[:, lo:lo + 2 * HEAD_DIM]
            tr = jnp.tanh(g[:, :HEAD_DIM])
            ti = jnp.tanh(g[:, HEAD_DIM:])
            hd = half_decay[d:d + 1]
            log_a = hd * tr + hd
            a = jnp.exp(log_a)
            m2 = jnp.tanh(log_a) * (-1.0 - a * a)
            m = jnp.where(m2 > 0.0, m2 * lax.rsqrt(m2), 0.0)
            b = m * ((ti + 1.0) * uh)
            a_ref[...] = a
            b_ref[...] = b

    def scan(c, bufs, carry, second_half):
        af_ref, bf_ref, ab_ref, bb_ref, hf_ref, hb_ref = bufs
        hf, hb = carry
        for t in range(LRU_CHUNK_T):
            of = t * BATCH
            ob = (LRU_CHUNK_T - 1 - t) * BATCH
            hf = af_ref[pl.ds(of, BATCH), :] * hf + bf_ref[pl.ds(of, BATCH), :]
            hb = ab_ref[pl.ds(ob, BATCH), :] * hb + bb_ref[pl.ds(ob, BATCH), :]
            hf_ref[pl.ds(of, BATCH), :] = hf
            hb_ref[pl.ds(ob, BATCH), :] = hb
        rf = chunk_row(c)
        rb = chunk_row(nc - 1 - c)
        if second_half:
            o_ref[pl.ds(rf, rc), :] = (acc_ref[pl.ds(rf, rc), :] + hf_ref[...]).astype(o_ref.dtype)
            o_ref[pl.ds(rb, rc), :] = (acc_ref[pl.ds(rb, rc), :] + hb_ref[...]).astype(o_ref.dtype)
        else:
            acc_ref[pl.ds(rf, rc), :] = hf_ref[...]
            acc_ref[pl.ds(rb, rc), :] = hb_ref[...]
        return hf, hb

    def pair(p, carry, second_half, last=False):
        c = 2 * p
        gates(c + 1, bufs1)
        carry = scan(c, bufs0, carry, second_half)
        if not last:
            gates(c + 2, bufs0)
        return scan(c + 1, bufs1, carry, second_half)

    h0 = jnp.zeros((BATCH, HEAD_DIM), F32)
    npair = nc // 2
    gates(0, bufs0)
    carry = lax.fori_loop(0, npair // 2, functools.partial(pair, second_half=False), (h0, h0))
    carry = lax.fori_loop(npair // 2, npair - 1, functools.partial(pair, second_half=True), carry)
    pair(npair - 1, carry, True, last=True)


def _lru(lx, cw, cb, wg, bg, lam):
    chunk = pltpu.VMEM((LRU_CHUNK_ROWS, HEAD_DIM), F32)
    return pl.pallas_call(
        _lru_kernel,
        grid=(LRU_HEADS,),
        in_specs=[
            pl.BlockSpec((None, ROWS, HEAD_DIM), lambda h: (h, 0, 0)),
            pl.BlockSpec((LRU_TAPS, HEAD_DIM), lambda h: (0, h)),
            pl.BlockSpec((1, HEAD_DIM), lambda h: (0, h)),
            pl.BlockSpec((1, HEAD_DIM, 4 * HEAD_DIM), lambda h: (h, 0, 0)),
            pl.BlockSpec((1, 1, 4 * HEAD_DIM), lambda h: (h, 0, 0)),
            pl.BlockSpec((1, 2, HEAD_DIM), lambda h: (h, 0, 0)),
        ],
        out_specs=pl.BlockSpec((None, ROWS, HEAD_DIM), lambda h: (h, 0, 0)),
        out_shape=jax.ShapeDtypeStruct((LRU_HEADS, ROWS, HEAD_DIM), BF16),
        scratch_shapes=[
            pltpu.VMEM((ROWS, HEAD_DIM), F32),
            pltpu.VMEM((ROWS, HEAD_DIM), F32),
        ] + [chunk] * 12,
        compiler_params=_params(("parallel",)),
        name="lru_mixer",
    )(lx, cw, cb, wg, bg, lam)


def _merge_kernel(w_ref, wprev_ref, wnext_ref, bg_ref, hl_ref, sgl_ref, sgc_ref,
                  scw_ref, scb_ref, wlo_ref, wco_ref, o_ref, z_ref, hcat_ref, res_ref):
    i = pl.program_id(0)
    tm = w_ref.shape[0]

    @pl.when(pl.program_id(1) == 0)
    def _():
        wc = w_ref[...]
        prev = jnp.where(i == 0, 0.0, wprev_ref[...])
        nxt = jnp.where(i == pl.num_programs(0) - 1, 0.0, wnext_ref[...])
        w_dn = jnp.concatenate([prev, wc[:tm - BATCH]], axis=0)
        w_up = jnp.concatenate([wc[BATCH:], nxt], axis=0)
        scw = scw_ref[...]
        v = scw[0:1] * w_dn + scw[1:2] * wc + scw[2:3] * w_up + scb_ref[...]
        z_ref[...] = (bg_ref[...].astype(F32) * v).astype(BF16)
        for h in range(LRU_HEADS):
            hcat_ref[:, h * HEAD_DIM:(h + 1) * HEAD_DIM] = hl_ref[h]

    y_lru = jnp.dot(hcat_ref[...], wlo_ref[...], preferred_element_type=F32)
    y_conv = jnp.dot(z_ref[...], wco_ref[...], preferred_element_type=F32)
    merged = sgl_ref[...].astype(F32) * y_lru + sgc_ref[...].astype(F32) * y_conv

    nslab = res_ref.shape[0]
    for k in range(nslab):
        res_ref[k] = merged[:, k * LANES:(k + 1) * LANES]
    for b in range(BATCH):
        for k in range(nslab):
            o_ref[b, :, k * LANES:(k + 1) * LANES] = (
                res_ref[k, pl.ds(b, tm // BATCH, stride=BATCH), :].astype(o_ref.dtype))


def _merge(w, bgate, hl, sg, scw, scb, wlo, wco, tm=512, tn=512):
    nb = tm // BATCH
    last = ROWS // BATCH - 1
    gc_off = D_MODEL // tn
    return pl.pallas_call(
        _merge_kernel,
        grid=(ROWS // tm, D_MODEL // tn),
        in_specs=[
            pl.BlockSpec((tm, CONV_WIDTH), lambda i, j: (i, 0)),
            pl.BlockSpec((BATCH, CONV_WIDTH), lambda i, j: (jnp.maximum(i * nb - 1, 0), 0)),
            pl.BlockSpec((BATCH, CONV_WIDTH), lambda i, j: (jnp.minimum((i + 1) * nb, last), 0)),
            pl.BlockSpec((tm, CONV_WIDTH), lambda i, j: (i, 0)),
            pl.BlockSpec((LRU_HEADS, tm, HEAD_DIM), lambda i, j: (0, i, 0)),
            pl.BlockSpec((tm, tn), lambda i, j: (i, j)),
            pl.BlockSpec((tm, tn), lambda i, j: (i, j + gc_off)),
            pl.BlockSpec((SC_TAPS, CONV_WIDTH), lambda i, j: (0, 0)),
            pl.BlockSpec((1, CONV_WIDTH), lambda i, j: (0, 0)),
            pl.BlockSpec((LRU_WIDTH, tn), lambda i, j: (0, j)),
            pl.BlockSpec((CONV_WIDTH, tn), lambda i, j: (0, j)),
        ],
        out_specs=pl.BlockSpec((BATCH, tm // BATCH, tn), lambda i, j: (0, i, j)),
        out_shape=jax.ShapeDtypeStruct((BATCH, SEQ, D_MODEL), BF16),
        scratch_shapes=[pltpu.VMEM((tm, CONV_WIDTH), BF16),
                        pltpu.VMEM((tm, LRU_WIDTH), BF16),
                        pltpu.VMEM((tn // LANES, tm, LANES), F32)],
        compiler_params=_params(("parallel", "arbitrary")),
        name="mixer_merge",
    )(w, w, w, bgate, hl, sg, sg, scw, scb, wlo, wco)


def _layer_norm(y, g, b):
    mu = jnp.mean(y, axis=-1, keepdims=True)
    d = y - mu
    var = jnp.mean(d * d, axis=-1, keepdims=True)
    return d * lax.rsqrt(var + LN_EPS) * g + b


def _outproj_kernel(m_ref, x_ref, wo_ref, g_ref, b_ref, o_ref):
    y = jnp.dot(m_ref[...], wo_ref[...], preferred_element_type=F32)
    o_ref[...] = _layer_norm(ALPHA * x_ref[...] + y, g_ref[...], b_ref[...])


def _outproj(merged, x2d, wo, g, b, tm=512):
    return pl.pallas_call(
        _outproj_kernel,
        grid=(ROWS // tm,),
        in_specs=[
            pl.BlockSpec((tm, D_MODEL), lambda i: (i, 0)),
            pl.BlockSpec((tm, D_MODEL), lambda i: (i, 0)),
            pl.BlockSpec((D_MODEL, D_MODEL), lambda i: (0, 0)),
            pl.BlockSpec((1, D_MODEL), lambda i: (0, 0)),
            pl.BlockSpec((1, D_MODEL), lambda i: (0, 0)),
        ],
        out_specs=pl.BlockSpec((tm, D_MODEL), lambda i: (i, 0)),
        out_shape=jax.ShapeDtypeStruct((ROWS, D_MODEL), F32),
        compiler_params=_params(("parallel",)),
        name="out_proj_ln1",
    )(merged, x2d, wo, g, b)


def _mlp_kernel(h_ref, w1_ref, b1_ref, w2_ref, b2_ref, g_ref, b_ref, o_ref, hb_ref):
    f = pl.program_id(1)

    @pl.when(f == 0)
    def _():
        h = h_ref[...]
        hb_ref[...] = h.astype(BF16)
        o_ref[...] = ALPHA * h + b2_ref[...]

    a = jnp.dot(hb_ref[...], w1_ref[...], preferred_element_type=F32) + b1_ref[...]
    a = jnp.maximum(a, 0.0)
    o_ref[...] += jnp.dot((a * a).astype(BF16), w2_ref[...], preferred_element_type=F32)

    @pl.when(f == pl.num_programs(1) - 1)
    def _():
        o_ref[...] = _layer_norm(o_ref[...], g_ref[...], b_ref[...])


def _mlp(h, w1, b1, w2, b2, g, b, tm=1024, tf=512):
    return pl.pallas_call(
        _mlp_kernel,
        grid=(ROWS // tm, D_FF // tf),
        in_specs=[
            pl.BlockSpec((tm, D_MODEL), lambda i, f: (i, 0)),
            pl.BlockSpec((D_MODEL, tf), lambda i, f: (0, f)),
            pl.BlockSpec((1, tf), lambda i, f: (0, f)),
            pl.BlockSpec((tf, D_MODEL), lambda i, f: (f, 0)),
            pl.BlockSpec((1, D_MODEL), lambda i, f: (0, 0)),
            pl.BlockSpec((1, D_MODEL), lambda i, f: (0, 0)),
            pl.BlockSpec((1, D_MODEL), lambda i, f: (0, 0)),
        ],
        out_specs=pl.BlockSpec((tm, D_MODEL), lambda i, f: (i, 0)),
        out_shape=jax.ShapeDtypeStruct((ROWS, D_MODEL), F32),
        scratch_shapes=[pltpu.VMEM((tm, D_MODEL), BF16)],
        compiler_params=_params(("parallel", "arbitrary")),
        name="mlp_ln2",
    )(h, w1, b1, w2, b2, g, b)


def kernel(x, w_in, lru_conv_w, lru_conv_b, lru_w_a, lru_b_a, lru_w_x, lru_b_x, lru_lambda,
           w_lru_out, sc_conv_w, sc_conv_b, w_conv_out, w_o, ln1_g, ln1_b, mlp_w1, mlp_b1,
           mlp_w2, mlp_b2, ln2_g, ln2_b):
    xt = jnp.transpose(x.astype(BF16), (1, 0, 2)).reshape(ROWS, D_MODEL)
    w_lru, w_cx, w_cb, w_cc, w_g = (
        w_in[:, lo:hi].astype(BF16)
        for lo, hi in zip((OFF_LRU, OFF_CX, OFF_CB, OFF_CC, OFF_GL), (OFF_CX, OFF_CB, OFF_CC, OFF_GL, IN_COLS)))

    lx = _proj(xt, (w_lru,), F32, "heads", tm=1024, tn=1280)
    cxc = _proj(xt, (w_cx, w_cc), F32, "prod", tm=2048, tn=512)
    cbg = _proj(xt, (w_cb,), BF16, "plain", tm=2048, tn=1024)
    sg = _proj(xt, (w_g,), BF16, "sigmoid", tm=2048, tn=1024)

    wg = jnp.concatenate([lru_w_a[0], lru_w_x[0], lru_w_a[1], lru_w_x[1]], axis=-1).astype(BF16)
    bg = jnp.concatenate([lru_b_a[0], lru_b_x[0], lru_b_a[1], lru_b_x[1]], axis=-1)
    bg = (0.5 * bg).reshape(LRU_HEADS, 1, 4 * HEAD_DIM)
    lam = lru_lambda.reshape(2, LRU_HEADS, HEAD_DIM).transpose(1, 0, 2)
    hl = _lru(lx, 0.5 * lru_conv_w, (0.5 * lru_conv_b).reshape(1, LRU_WIDTH), wg, bg, lam)

    merged = _merge(cxc, cbg, hl, sg, sc_conv_w, sc_conv_b.reshape(1, CONV_WIDTH),
                    w_lru_out.astype(BF16), w_conv_out.astype(BF16))
    h1 = _outproj(merged.reshape(ROWS, D_MODEL), x.reshape(ROWS, D_MODEL), w_o.astype(BF16),
                  ln1_g.reshape(1, D_MODEL), ln1_b.reshape(1, D_MODEL))
    h2 = _mlp(h1, mlp_w1.astype(BF16), mlp_b1.reshape(1, D_FF), mlp_w2.astype(BF16),
              mlp_b2.reshape(1, D_MODEL), ln2_g.reshape(1, D_MODEL), ln2_b.reshape(1, D_MODEL))
    return h2.reshape(BATCH, SEQ, D_MODEL)
```

```python
import functools

import jax
import jax.numpy as jnp
from jax import lax
from jax.experimental import pallas as pl
from jax.experimental.pallas import tpu as pltpu

F32 = jnp.float32
BF16 = jnp.bfloat16

D_MODEL = 2048
BATCH = 8
SEQ = 2048
ROWS = BATCH * SEQ
LRU_WIDTH = 2560
LRU_HEADS = 20
HEAD_DIM = 128
LRU_TAPS = 4
LRU_LEFT = 2
LRU_C = 8.0
CONV_WIDTH = 2048
SC_TAPS = 3
D_FF = 4 * D_MODEL
ALPHA = float(2.0 ** 0.25)
LN_EPS = 1e-5
LANES = 128

OFF_LRU = 0
OFF_CX = LRU_WIDTH
OFF_CB = OFF_CX + CONV_WIDTH
OFF_CC = OFF_CB + CONV_WIDTH
OFF_GL = OFF_CC + CONV_WIDTH
IN_COLS = OFF_GL + 2 * D_MODEL

VMEM_LIMIT = 56 * 1024 * 1024


def _params(sem):
    return pltpu.CompilerParams(dimension_semantics=sem, vmem_limit_bytes=VMEM_LIMIT)


def _to_time_major_kernel(x_ref, o_ref, slab_ref):
    nslab, rows, _ = slab_ref.shape
    t = rows // BATCH
    for g in range(0, D_MODEL // LANES, nslab):
        for k in range(nslab):
            lo = (g + k) * LANES
            for b in range(BATCH):
                slab_ref[k, pl.ds(b, t, stride=BATCH), :] = x_ref[b, :, lo:lo + LANES]
        for k in range(nslab):
            lo = (g + k) * LANES
            o_ref[:, lo:lo + LANES] = slab_ref[k].astype(o_ref.dtype)


def _to_time_major(x, t=128, nslab=4):
    return pl.pallas_call(
        _to_time_major_kernel,
        grid=(SEQ // t,),
        in_specs=[pl.BlockSpec((BATCH, t, D_MODEL), lambda i: (0, i, 0))],
        out_specs=pl.BlockSpec((t * BATCH, D_MODEL), lambda i: (i, 0)),
        out_shape=jax.ShapeDtypeStruct((ROWS, D_MODEL), BF16),
        scratch_shapes=[pltpu.VMEM((nslab, t * BATCH, LANES), F32)],
        compiler_params=_params(("parallel",)),
        name="to_time_major",
    )(x)


PROJ_COLS = 512


def _proj_kernel(x_ref, *refs, kind, nblk):
    o_ref = refs[-1]
    for q in range(nblk):
        acc = jnp.dot(x_ref[...], refs[q][...], preferred_element_type=F32)
        if kind == "prod":
            acc = acc * jnp.dot(x_ref[...], refs[nblk + q][...], preferred_element_type=F32)
        elif kind == "sigmoid":
            acc = jax.nn.sigmoid(acc)
        if kind == "heads":
            nh = PROJ_COLS // HEAD_DIM
            for h in range(nh):
                o_ref[q * nh + h] = acc[:, h * HEAD_DIM:(h + 1) * HEAD_DIM]
        else:
            o_ref[:, q * PROJ_COLS:(q + 1) * PROJ_COLS] = acc.astype(o_ref.dtype)


def _proj(xt, w, col_offs, ncols, out_dtype, kind, tm, nblk):
    k = xt.shape[1]
    tn = nblk * PROJ_COLS
    assert ncols % tn == 0 and ROWS % tm == 0 and all(off % PROJ_COLS == 0 for off in col_offs)
    w_specs = [
        pl.BlockSpec((k, PROJ_COLS),
                     functools.partial(lambda i, j, base: (0, base + j * nblk), base=off // PROJ_COLS + q))
        for off in col_offs for q in range(nblk)
    ]
    if kind == "heads":
        nh = tn // HEAD_DIM
        out_spec = pl.BlockSpec((nh, tm, HEAD_DIM), lambda i, j: (j, i, 0))
        out_shape = jax.ShapeDtypeStruct((ncols // HEAD_DIM, ROWS, HEAD_DIM), out_dtype)
    else:
        out_spec = pl.BlockSpec((tm, tn), lambda i, j: (i, j))
        out_shape = jax.ShapeDtypeStruct((ROWS, ncols), out_dtype)
    return pl.pallas_call(
        functools.partial(_proj_kernel, kind=kind, nblk=nblk),
        grid=(ROWS // tm, ncols // tn),
        in_specs=[pl.BlockSpec((tm, k), lambda i, j: (i, 0))] + w_specs,
        out_specs=out_spec,
        out_shape=out_shape,
        compiler_params=_params(("parallel", "arbitrary")),
        name="proj_" + kind,
    )(xt, *([w] * len(w_specs)))


LRU_CHUNK_T = 128
LRU_CHUNK_ROWS = LRU_CHUNK_T * BATCH
LRU_NCHUNK = SEQ // LRU_CHUNK_T


def _softplus(x):
    return jnp.maximum(x, 0.0) + jnp.log1p(jnp.exp(-jnp.abs(x)))


def _lru_kernel(lx_ref, cw_ref, cb_ref, wg_ref, bg_ref, lam_ref, o_ref, u_ref, acc_ref, *chunk_refs):
    bufs0, bufs1 = chunk_refs[:6], chunk_refs[6:]
    rc = LRU_CHUNK_ROWS
    nc = LRU_NCHUNK
    cw = cw_ref[...]
    cb = cb_ref[...]

    def conv_chunk(r0, first, last):
        u = cb
        for k in range(LRU_TAPS):
            off = (k - LRU_LEFT) * BATCH
            if first and off < 0:
                tap = jnp.concatenate(
                    [jnp.zeros((-off, HEAD_DIM), F32), lx_ref[pl.ds(0, rc + off), :]], axis=0)
            elif last and off > 0:
                tap = jnp.concatenate(
                    [lx_ref[pl.ds(r0 + off, rc - off), :], jnp.zeros((off, HEAD_DIM), F32)], axis=0)
            else:
                tap = lx_ref[pl.ds(r0 + off, rc), :]
            u = u + cw[k:k + 1] * tap
        u_ref[pl.ds(r0, rc), :] = u

    conv_chunk(0, True, False)

    def conv_body(c, carry):
        conv_chunk(pl.multiple_of(c * rc, rc), False, False)
        return carry

    lax.fori_loop(1, nc - 1, conv_body, 0)
    conv_chunk(ROWS - rc, False, True)

    half_decay = (-0.5 * LRU_C) * _softplus(-lam_ref[0])
    wg = wg_ref[0]
    bg = bg_ref[0]

    def chunk_row(c):
        return c * rc if isinstance(c, int) else pl.multiple_of(c * rc, rc)

    def gates(c, bufs):
        af_ref, bf_ref, ab_ref, bb_ref = bufs[:4]
        for d, (a_ref, b_ref) in enumerate(((af_ref, bf_ref), (ab_ref, bb_ref))):
            r0 = chunk_row(c if d == 0 else nc - 1 - c)
            uh = u_ref[pl.ds(r0, rc), :]
            lo = 2 * HEAD_DIM * d
            g = jnp.dot(uh.astype(BF16), wg[:, lo:lo + 2 * HEAD_DIM],
                        preferred_element_type=F32) + bg[:, lo:lo + 2 * HEAD_DIM]
            tr = jnp.tanh(g[:, :HEAD_DIM])
            ti = jnp.tanh(g[:, HEAD_DIM:])
            hd = half_decay[d:d + 1]
            log_a = hd * tr + hd
            a = jnp.exp(log_a)
            m2 = jnp.tanh(log_a) * (-1.0 - a * a)
            m = jnp.where(m2 > 0.0, m2 * lax.rsqrt(m2), 0.0)
            b = m * ((ti + 1.0) * uh)
            a_ref[...] = a
            b_ref[...] = b

    def scan(c, bufs, carry, second_half):
        af_ref, bf_ref, ab_ref, bb_ref, hf_ref, hb_ref = bufs
        hf, hb = carry
        for t in range(LRU_CHUNK_T):
            of = t * BATCH
            ob = (LRU_CHUNK_T - 1 - t) * BATCH
            hf = af_ref[pl.ds(of, BATCH), :] * hf + bf_ref[pl.ds(of, BATCH), :]
            hb = ab_ref[pl.ds(ob, BATCH), :] * hb + bb_ref[pl.ds(ob, BATCH), :]
            hf_ref[pl.ds(of, BATCH), :] = hf
            hb_ref[pl.ds(ob, BATCH), :] = hb
        rf = chunk_row(c)
        rb = chunk_row(nc - 1 - c)
        if second_half:
            o_ref[pl.ds(rf, rc), :] = (acc_ref[pl.ds(rf, rc), :] + hf_ref[...]).astype(o_ref.dtype)
            o_ref[pl.ds(rb, rc), :] = (acc_ref[pl.ds(rb, rc), :] + hb_ref[...]).astype(o_ref.dtype)
        else:
            acc_ref[pl.ds(rf, rc), :] = hf_ref[...]
            acc_ref[pl.ds(rb, rc), :] = hb_ref[...]
        return hf, hb

    def pair(p, carry, second_half, last=False):
        c = 2 * p
        gates(c + 1, bufs1)
        carry = scan(c, bufs0, carry, second_half)
        if not last:
            gates(c + 2, bufs0)
        return scan(c + 1, bufs1, carry, second_half)

    h0 = jnp.zeros((BATCH, HEAD_DIM), F32)
    npair = nc // 2
    gates(0, bufs0)
    carry = lax.fori_loop(0, npair // 2, functools.partial(pair, second_half=False), (h0, h0))
    carry = lax.fori_loop(npair // 2, npair - 1, functools.partial(pair, second_half=True), carry)
    pair(npair - 1, carry, True, last=True)


def _lru(lx, cw, cb, wg, bg, lam):
    chunk = pltpu.VMEM((LRU_CHUNK_ROWS, HEAD_DIM), F32)
    return pl.pallas_call(
        _lru_kernel,
        grid=(LRU_HEADS,),
        in_specs=[
            pl.BlockSpec((None, ROWS, HEAD_DIM), lambda h: (h, 0, 0)),
            pl.BlockSpec((LRU_TAPS, HEAD_DIM), lambda h: (0, h)),
            pl.BlockSpec((1, HEAD_DIM), lambda h: (0, h)),
            pl.BlockSpec((1, HEAD_DIM, 4 * HEAD_DIM), lambda h: (h, 0, 0)),
            pl.BlockSpec((1, 1, 4 * HEAD_DIM), lambda h: (h, 0, 0)),
            pl.BlockSpec((1, 2, HEAD_DIM), lambda h: (h, 0, 0)),
        ],
        out_specs=pl.BlockSpec((None, ROWS, HEAD_DIM), lambda h: (h, 0, 0)),
        out_shape=jax.ShapeDtypeStruct((LRU_HEADS, ROWS, HEAD_DIM), BF16),
        scratch_shapes=[
            pltpu.VMEM((ROWS, HEAD_DIM), F32),
            pltpu.VMEM((ROWS, HEAD_DIM), F32),
        ] + [chunk] * 12,
        compiler_params=_params(("parallel",)),
        name="lru_mixer",
    )(lx, cw, cb, wg, bg, lam)


def _merge_kernel(w_ref, wprev_ref, wnext_ref, bg_ref, hl_ref, sgl_ref, sgc_ref,
                  scw_ref, scb_ref, wlo_ref, wco_ref, o_ref, z_ref, hcat_ref, res_ref):
    i = pl.program_id(0)
    tm = w_ref.shape[0]

    @pl.when(pl.program_id(1) == 0)
    def _():
        wc = w_ref[...]
        prev = jnp.where(i == 0, 0.0, wprev_ref[...])
        nxt = jnp.where(i == pl.num_programs(0) - 1, 0.0, wnext_ref[...])
        w_dn = jnp.concatenate([prev, wc[:tm - BATCH]], axis=0)
        w_up = jnp.concatenate([wc[BATCH:], nxt], axis=0)
        scw = scw_ref[...]
        v = scw[0:1] * w_dn + scw[1:2] * wc + scw[2:3] * w_up + scb_ref[...]
        z_ref[...] = (bg_ref[...].astype(F32) * v).astype(BF16)
        for h in range(LRU_HEADS):
            hcat_ref[:, h * HEAD_DIM:(h + 1) * HEAD_DIM] = hl_ref[h]

    y_lru = jnp.dot(hcat_ref[...], wlo_ref[...], preferred_element_type=F32)
    y_conv = jnp.dot(z_ref[...], wco_ref[...], preferred_element_type=F32)
    merged = sgl_ref[...].astype(F32) * y_lru + sgc_ref[...].astype(F32) * y_conv

    nslab = res_ref.shape[0]
    for k in range(nslab):
        res_ref[k] = merged[:, k * LANES:(k + 1) * LANES]
    for b in range(BATCH):
        for k in range(nslab):
            o_ref[b, :, k * LANES:(k + 1) * LANES] = (
                res_ref[k, pl.ds(b, tm // BATCH, stride=BATCH), :].astype(o_ref.dtype))


def _merge(w, bgate, hl, sg, scw, scb, wlo, wco, tm=512, tn=512):
    nb = tm // BATCH
    last = ROWS // BATCH - 1
    gc_off = D_MODEL // tn
    return pl.pallas_call(
        _merge_kernel,
        grid=(ROWS // tm, D_MODEL // tn),
        in_specs=[
            pl.BlockSpec((tm, CONV_WIDTH), lambda i, j: (i, 0)),
            pl.BlockSpec((BATCH, CONV_WIDTH), lambda i, j: (jnp.maximum(i * nb - 1, 0), 0)),
            pl.BlockSpec((BATCH, CONV_WIDTH), lambda i, j: (jnp.minimum((i + 1) * nb, last), 0)),
            pl.BlockSpec((tm, CONV_WIDTH), lambda i, j: (i, 0)),
            pl.BlockSpec((LRU_HEADS, tm, HEAD_DIM), lambda i, j: (0, i, 0)),
            pl.BlockSpec((tm, tn), lambda i, j: (i, j)),
            pl.BlockSpec((tm, tn), lambda i, j: (i, j + gc_off)),
            pl.BlockSpec((SC_TAPS, CONV_WIDTH), lambda i, j: (0, 0)),
            pl.BlockSpec((1, CONV_WIDTH), lambda i, j: (0, 0)),
            pl.BlockSpec((LRU_WIDTH, tn), lambda i, j: (0, j)),
            pl.BlockSpec((CONV_WIDTH, tn), lambda i, j: (0, j)),
        ],
        out_specs=pl.BlockSpec((BATCH, tm // BATCH, tn), lambda i, j: (0, i, j)),
        out_shape=jax.ShapeDtypeStruct((BATCH, SEQ, D_MODEL), BF16),
        scratch_shapes=[pltpu.VMEM((tm, CONV_WIDTH), BF16),
                        pltpu.VMEM((tm, LRU_WIDTH), BF16),
                        pltpu.VMEM((tn // LANES, tm, LANES), F32)],
        compiler_params=_params(("parallel", "arbitrary")),
        name="mixer_merge",
    )(w, w, w, bgate, hl, sg, sg, scw, scb, wlo, wco)


def _layer_norm(y, g, b):
    mu = jnp.mean(y, axis=-1, keepdims=True)
    d = y - mu
    var = jnp.mean(d * d, axis=-1, keepdims=True)
    return d * lax.rsqrt(var + LN_EPS) * g + b


OUTPROJ_SUB_ROWS = 256


def _outproj_kernel(m_ref, x_ref, wo_ref, g_ref, b_ref, o_ref):
    sub = OUTPROJ_SUB_ROWS
    for r in range(0, m_ref.shape[0], sub):
        y = jnp.dot(m_ref[r:r + sub, :], wo_ref[...], preferred_element_type=F32)
        o_ref[r:r + sub, :] = _layer_norm(ALPHA * x_ref[r:r + sub, :] + y, g_ref[...], b_ref[...])


def _outproj(merged, x2d, wo, g, b, tm=512):
    return pl.pallas_call(
        _outproj_kernel,
        grid=(ROWS // tm,),
        in_specs=[
            pl.BlockSpec((tm, D_MODEL), lambda i: (i, 0)),
            pl.BlockSpec((tm, D_MODEL), lambda i: (i, 0)),
            pl.BlockSpec((D_MODEL, D_MODEL), lambda i: (0, 0)),
            pl.BlockSpec((1, D_MODEL), lambda i: (0, 0)),
            pl.BlockSpec((1, D_MODEL), lambda i: (0, 0)),
        ],
        out_specs=pl.BlockSpec((tm, D_MODEL), lambda i: (i, 0)),
        out_shape=jax.ShapeDtypeStruct((ROWS, D_MODEL), F32),
        compiler_params=_params(("parallel",)),
        name="out_proj_ln1",
    )(merged, x2d, wo, g, b)


def _mlp_kernel(h_ref, w1_ref, b1_ref, w2_ref, b2_ref, g_ref, b_ref, o_ref, hb_ref):
    f = pl.program_id(1)

    @pl.when(f == 0)
    def _():
        h = h_ref[...]
        hb_ref[...] = h.astype(BF16)
        o_ref[...] = ALPHA * h + b2_ref[...]

    a = jnp.dot(hb_ref[...], w1_ref[...], preferred_element_type=F32) + b1_ref[...]
    a = jnp.maximum(a, 0.0)
    o_ref[...] += jnp.dot((a * a).astype(BF16), w2_ref[...], preferred_element_type=F32)

    @pl.when(f == pl.num_programs(1) - 1)
    def _():
        o_ref[...] = _layer_norm(o_ref[...], g_ref[...], b_ref[...])


def _mlp(h, w1, b1, w2, b2, g, b, tm=512, tf=1024):
    return pl.pallas_call(
        _mlp_kernel,
        grid=(ROWS // tm, D_FF // tf),
        in_specs=[
            pl.BlockSpec((tm, D_MODEL), lambda i, f: (i, 0)),
            pl.BlockSpec((D_MODEL, tf), lambda i, f: (0, f)),
            pl.BlockSpec((1, tf), lambda i, f: (0, f)),
            pl.BlockSpec((tf, D_MODEL), lambda i, f: (f, 0)),
            pl.BlockSpec((1, D_MODEL), lambda i, f: (0, 0)),
            pl.BlockSpec((1, D_MODEL), lambda i, f: (0, 0)),
            pl.BlockSpec((1, D_MODEL), lambda i, f: (0, 0)),
        ],
        out_specs=pl.BlockSpec((tm, D_MODEL), lambda i, f: (i, 0)),
        out_shape=jax.ShapeDtypeStruct((ROWS, D_MODEL), F32),
        scratch_shapes=[pltpu.VMEM((tm, D_MODEL), BF16)],
        compiler_params=_params(("parallel", "arbitrary")),
        name="mlp_ln2",
    )(h, w1, b1, w2, b2, g, b)


def kernel(x, w_in, lru_conv_w, lru_conv_b, lru_w_a, lru_b_a, lru_w_x, lru_b_x, lru_lambda,
           w_lru_out, sc_conv_w, sc_conv_b, w_conv_out, w_o, ln1_g, ln1_b, mlp_w1, mlp_b1,
           mlp_w2, mlp_b2, ln2_g, ln2_b):
    xt = _to_time_major(x)
    w_in_b = w_in.astype(BF16)

    lx = _proj(xt, w_in_b, (OFF_LRU,), LRU_WIDTH, F32, "heads", tm=1024, nblk=5)
    cxc = _proj(xt, w_in_b, (OFF_CX, OFF_CC), CONV_WIDTH, F32, "prod", tm=2048, nblk=1)
    cbg = _proj(xt, w_in_b, (OFF_CB,), CONV_WIDTH, BF16, "plain", tm=2048, nblk=2)
    sg = _proj(xt, w_in_b, (OFF_GL,), 2 * D_MODEL, BF16, "sigmoid", tm=2048, nblk=2)

    wg = jnp.concatenate([lru_w_a[0], lru_w_x[0], lru_w_a[1], lru_w_x[1]], axis=-1).astype(BF16)
    bg = jnp.concatenate([lru_b_a[0], lru_b_x[0], lru_b_a[1], lru_b_x[1]], axis=-1)
    bg = (0.5 * bg).reshape(LRU_HEADS, 1, 4 * HEAD_DIM)
    lam = lru_lambda.reshape(2, LRU_HEADS, HEAD_DIM).transpose(1, 0, 2)
    hl = _lru(lx, 0.5 * lru_conv_w, (0.5 * lru_conv_b).reshape(1, LRU_WIDTH), wg, bg, lam)

    merged = _merge(cxc, cbg, hl, sg, sc_conv_w, sc_conv_b.reshape(1, CONV_WIDTH),
                    w_lru_out.astype(BF16), w_conv_out.astype(BF16))
    h1 = _outproj(merged.reshape(ROWS, D_MODEL), x.reshape(ROWS, D_MODEL), w_o.astype(BF16),
                  ln1_g.reshape(1, D_MODEL), ln1_b.reshape(1, D_MODEL))
    h2 = _mlp(h1, mlp_w1.astype(BF16), mlp_b1.reshape(1, D_FF), mlp_w2.astype(BF16),
              mlp_b2.reshape(1, D_MODEL), ln2_g.reshape(1, D_MODEL), ln2_b.reshape(1, D_MODEL))
    return h2.reshape(BATCH, SEQ, D_MODEL)
```

```python
import functools

import jax
import jax.numpy as jnp
from jax import lax
from jax.experimental import pallas as pl
from jax.experimental.pallas import tpu as pltpu

F32 = jnp.float32
BF16 = jnp.bfloat16

D_MODEL = 2048
BATCH = 8
SEQ = 2048
ROWS = BATCH * SEQ
LRU_WIDTH = 2560
LRU_HEADS = 20
HEAD_DIM = 128
LRU_TAPS = 4
LRU_LEFT = 2
LRU_C = 8.0
CONV_WIDTH = 2048
SC_TAPS = 3
D_FF = 4 * D_MODEL
ALPHA = float(2.0 ** 0.25)
LN_EPS = 1e-5
LANES = 128

OFF_LRU = 0
OFF_CX = LRU_WIDTH
OFF_CB = OFF_CX + CONV_WIDTH
OFF_CC = OFF_CB + CONV_WIDTH
OFF_GL = OFF_CC + CONV_WIDTH
IN_COLS = OFF_GL + 2 * D_MODEL

VMEM_LIMIT = 56 * 1024 * 1024


def _params(sem):
    return pltpu.CompilerParams(dimension_semantics=sem, vmem_limit_bytes=VMEM_LIMIT)


def _to_time_major_kernel(x_ref, o_ref, slab_ref):
    nslab, rows, _ = slab_ref.shape
    t = rows // BATCH
    for g in range(0, D_MODEL // LANES, nslab):
        for k in range(nslab):
            lo = (g + k) * LANES
            for b in range(BATCH):
                slab_ref[k, pl.ds(b, t, stride=BATCH), :] = x_ref[b, :, lo:lo + LANES]
        for k in range(nslab):
            lo = (g + k) * LANES
            o_ref[:, lo:lo + LANES] = slab_ref[k].astype(o_ref.dtype)


def _to_time_major(x, t=128, nslab=4):
    return pl.pallas_call(
        _to_time_major_kernel,
        grid=(SEQ // t,),
        in_specs=[pl.BlockSpec((BATCH, t, D_MODEL), lambda i: (0, i, 0))],
        out_specs=pl.BlockSpec((t * BATCH, D_MODEL), lambda i: (i, 0)),
        out_shape=jax.ShapeDtypeStruct((ROWS, D_MODEL), BF16),
        scratch_shapes=[pltpu.VMEM((nslab, t * BATCH, LANES), F32)],
        compiler_params=_params(("parallel",)),
        name="to_time_major",
    )(x)


PROJ_COLS = 512
BLK_LRU, BLK_CX, BLK_CB, BLK_CC, BLK_GL = (
    off // PROJ_COLS for off in (OFF_LRU, OFF_CX, OFF_CB, OFF_CC, OFF_GL))


def _proj_heads_kernel(x_ref, *refs):
    o_ref = refs[-1]
    nh = PROJ_COLS // HEAD_DIM
    for q, w_ref in enumerate(refs[:-1]):
        acc = jnp.dot(x_ref[...], w_ref[...], preferred_element_type=F32)
        for h in range(nh):
            o_ref[q * nh + h] = acc[:, h * HEAD_DIM:(h + 1) * HEAD_DIM]


def _proj_heads(xt, w, tm=1024):
    nblk = LRU_WIDTH // PROJ_COLS
    w_specs = [
        pl.BlockSpec((D_MODEL, PROJ_COLS), functools.partial(lambda i, q: (0, q), q=BLK_LRU + q))
        for q in range(nblk)
    ]
    return pl.pallas_call(
        _proj_heads_kernel,
        grid=(ROWS // tm,),
        in_specs=[pl.BlockSpec((tm, D_MODEL), lambda i: (i, 0))] + w_specs,
        out_specs=pl.BlockSpec((LRU_HEADS, tm, HEAD_DIM), lambda i: (0, i, 0)),
        out_shape=jax.ShapeDtypeStruct((LRU_HEADS, ROWS, HEAD_DIM), F32),
        compiler_params=_params(("parallel",)),
        name="proj_heads",
    )(xt, *([w] * nblk))


MIX_TM = 1024
MIX_ROW_TILES = ROWS // MIX_TM
LRU_BLOCK_T = 256
LRU_BLOCK_ROWS = LRU_BLOCK_T * BATCH
LRU_NBLOCK = SEQ // LRU_BLOCK_T
LRU_HALO_TOP = 16
LRU_HALO_BOT = 8
MIX_STEPS_G = MIX_ROW_TILES * 4
MIX_STEPS_B = MIX_ROW_TILES * 2
MIX_STEPS_X = MIX_ROW_TILES * 4
MIX_STEPS = MIX_STEPS_G + MIX_STEPS_B + MIX_STEPS_X
assert MIX_STEPS == LRU_HEADS * LRU_NBLOCK


def _mix_section(s):
    in_g = s < MIX_STEPS_G
    in_b = jnp.logical_and(s >= MIX_STEPS_G, s < MIX_STEPS_G + MIX_STEPS_B)
    t = jnp.where(in_g, s, jnp.where(in_b, s - MIX_STEPS_G, s - MIX_STEPS_G - MIX_STEPS_B))
    i = jnp.where(in_b, t // 2, t // 4)
    c = jnp.where(in_b, t % 2, t % 4)
    wa = jnp.where(in_g, BLK_GL + 2 * c, jnp.where(in_b, BLK_CB + 2 * c, BLK_CX + c))
    wb = jnp.where(jnp.logical_or(in_g, in_b), wa + 1, BLK_CC + c)
    return in_g, in_b, i, c, wa, wb


def _softplus(x):
    return jnp.maximum(x, 0.0) + jnp.log1p(jnp.exp(-jnp.abs(x)))


def _mix_kernel(x_ref, wa_ref, wb_ref, lf_ref, lfp_ref, lfn_ref, lb_ref, lbp_ref, lbn_ref,
                cw_ref, cb_ref, wg_ref, bg_ref, lam_ref,
                sg_ref, cbg_ref, cxc_ref, hf_ref, hb_ref,
                raw_ref, u_ref, a_ref, b_ref, st_ref, carry_ref):
    s = pl.program_id(0)
    p = s % LRU_NBLOCK
    n = LRU_BLOCK_ROWS

    @pl.when(p == 0)
    def _():
        carry_ref[...] = jnp.zeros_like(carry_ref)

    x = x_ref[...]
    half = PROJ_COLS // 2
    raw_ref[0, :, :half] = jnp.dot(x, wa_ref[:, :half], preferred_element_type=F32)

    cw = cw_ref[...]
    cb = cb_ref[...]

    def conv(d, main_ref, top_ref, bot_ref, is_first, is_last):
        top = jnp.where(is_first, 0.0, top_ref[...])
        bot = jnp.where(is_last, 0.0, bot_ref[...])
        u = cb + cw[LRU_LEFT:LRU_LEFT + 1] * main_ref[...]
        for k in range(LRU_TAPS):
            off = (k - LRU_LEFT) * BATCH
            if off < 0:
                tap = jnp.concatenate([top[LRU_HALO_TOP + off:], main_ref[pl.ds(0, n + off), :]], axis=0)
            elif off > 0:
                tap = jnp.concatenate([main_ref[pl.ds(off, n - off), :], bot[:off]], axis=0)
            else:
                continue
            u = u + cw[k:k + 1] * tap
        u_ref[d] = u

    conv(0, lf_ref, lfp_ref, lfn_ref, p == 0, p == LRU_NBLOCK - 1)
    conv(1, lb_ref, lbp_ref, lbn_ref, p == LRU_NBLOCK - 1, p == 0)

    half_decay = (-0.5 * LRU_C) * _softplus(-lam_ref[0])
    wg = wg_ref[0]
    bg = bg_ref[0]
    gate_pre = []
    for d in range(2):
        lo = 2 * HEAD_DIM * d
        gate_pre.append(jnp.dot(u_ref[d].astype(BF16), wg[:, lo:lo + 2 * HEAD_DIM],
                                preferred_element_type=F32) + bg[:, lo:lo + 2 * HEAD_DIM])

    raw_ref[0, :, half:] = jnp.dot(x, wa_ref[:, half:], preferred_element_type=F32)
    raw_ref[1] = jnp.dot(x, wb_ref[...], preferred_element_type=F32)

    for d in range(2):
        uh = u_ref[d]
        g = gate_pre[d]
        tr = jnp.tanh(g[:, :HEAD_DIM])
        ti = jnp.tanh(g[:, HEAD_DIM:])
        hd = half_decay[d:d + 1]
        log_a = hd * tr + hd
        a = jnp.exp(log_a)
        m2 = jnp.tanh(log_a) * (-1.0 - a * a)
        m = jnp.where(m2 > 0.0, m2 * lax.rsqrt(m2), 0.0)
        a_ref[d] = a
        b_ref[d] = m * ((ti + 1.0) * uh)

    hf = carry_ref[0]
    hb = carry_ref[1]
    for t in range(LRU_BLOCK_T):
        of = t * BATCH
        ob = (LRU_BLOCK_T - 1 - t) * BATCH
        hf = a_ref[0, pl.ds(of, BATCH), :] * hf + b_ref[0, pl.ds(of, BATCH), :]
        hb = a_ref[1, pl.ds(ob, BATCH), :] * hb + b_ref[1, pl.ds(ob, BATCH), :]
        st_ref[0, pl.ds(of, BATCH), :] = hf
        st_ref[1, pl.ds(ob, BATCH), :] = hb
    carry_ref[0] = hf
    carry_ref[1] = hb
    hf_ref[...] = st_ref[0].astype(hf_ref.dtype)
    hb_ref[...] = st_ref[1].astype(hb_ref.dtype)

    @pl.when(s < MIX_STEPS_G)
    def _():
        sg_ref[:, :PROJ_COLS] = (0.5 * jnp.tanh(0.5 * raw_ref[0]) + 0.5).astype(sg_ref.dtype)
        sg_ref[:, PROJ_COLS:] = (0.5 * jnp.tanh(0.5 * raw_ref[1]) + 0.5).astype(sg_ref.dtype)

    @pl.when(jnp.logical_and(s >= MIX_STEPS_G, s < MIX_STEPS_G + MIX_STEPS_B))
    def _():
        cbg_ref[:, :PROJ_COLS] = raw_ref[0].astype(cbg_ref.dtype)
        cbg_ref[:, PROJ_COLS:] = raw_ref[1].astype(cbg_ref.dtype)

    @pl.when(s >= MIX_STEPS_G + MIX_STEPS_B)
    def _():
        cxc_ref[...] = raw_ref[0] * raw_ref[1]


def _mix(xt, w, lx, cw, cb, wg, bg, lam):
    nb = LRU_NBLOCK
    top_per_block = LRU_BLOCK_ROWS // LRU_HALO_TOP
    bot_per_block = LRU_BLOCK_ROWS // LRU_HALO_BOT
    last_bot = ROWS // LRU_HALO_BOT - 1

    def head(s):
        return s // nb

    def fwd(s):
        return s % nb

    def bwd(s):
        return nb - 1 - s % nb

    def main_spec(blk):
        return pl.BlockSpec((None, LRU_BLOCK_ROWS, HEAD_DIM), lambda s: (head(s), blk(s), 0))

    def top_spec(blk):
        return pl.BlockSpec((None, LRU_HALO_TOP, HEAD_DIM),
                            lambda s: (head(s), jnp.maximum(blk(s) * top_per_block - 1, 0), 0))

    def bot_spec(blk):
        return pl.BlockSpec((None, LRU_HALO_BOT, HEAD_DIM),
                            lambda s: (head(s), jnp.minimum((blk(s) + 1) * bot_per_block, last_bot), 0))

    def out_g(s):
        in_g, _, i, c, _, _ = _mix_section(s)
        return jnp.where(in_g, i, MIX_ROW_TILES - 1), jnp.where(in_g, c, 3)

    def out_b(s):
        in_g, in_b, i, c, _, _ = _mix_section(s)
        return (jnp.where(in_b, i, jnp.where(in_g, 0, MIX_ROW_TILES - 1)),
                jnp.where(in_b, c, jnp.where(in_g, 0, 1)))

    def out_x(s):
        in_g, in_b, i, c, _, _ = _mix_section(s)
        before = jnp.logical_or(in_g, in_b)
        return jnp.where(before, 0, i), jnp.where(before, 0, c)

    block = pltpu.VMEM((2, LRU_BLOCK_ROWS, HEAD_DIM), F32)
    head_major = jax.ShapeDtypeStruct((LRU_HEADS, ROWS, HEAD_DIM), BF16)
    return pl.pallas_call(
        _mix_kernel,
        grid=(MIX_STEPS,),
        in_specs=[
            pl.BlockSpec((MIX_TM, D_MODEL), lambda s: (_mix_section(s)[2], 0)),
            pl.BlockSpec((D_MODEL, PROJ_COLS), lambda s: (0, _mix_section(s)[4])),
            pl.BlockSpec((D_MODEL, PROJ_COLS), lambda s: (0, _mix_section(s)[5])),
            main_spec(fwd), top_spec(fwd), bot_spec(fwd),
            main_spec(bwd), top_spec(bwd), bot_spec(bwd),
            pl.BlockSpec((LRU_TAPS, HEAD_DIM), lambda s: (0, head(s))),
            pl.BlockSpec((1, HEAD_DIM), lambda s: (0, head(s))),
            pl.BlockSpec((1, HEAD_DIM, 4 * HEAD_DIM), lambda s: (head(s), 0, 0)),
            pl.BlockSpec((1, 1, 4 * HEAD_DIM), lambda s: (head(s), 0, 0)),
            pl.BlockSpec((1, 2, HEAD_DIM), lambda s: (head(s), 0, 0)),
        ],
        out_specs=[
            pl.BlockSpec((MIX_TM, 2 * PROJ_COLS), out_g),
            pl.BlockSpec((MIX_TM, 2 * PROJ_COLS), out_b),
            pl.BlockSpec((MIX_TM, PROJ_COLS), out_x),
            pl.BlockSpec((None, LRU_BLOCK_ROWS, HEAD_DIM), lambda s: (head(s), fwd(s), 0)),
            pl.BlockSpec((None, LRU_BLOCK_ROWS, HEAD_DIM), lambda s: (head(s), bwd(s), 0)),
        ],
        out_shape=[
            jax.ShapeDtypeStruct((ROWS, 2 * D_MODEL), BF16),
            jax.ShapeDtypeStruct((ROWS, CONV_WIDTH), BF16),
            jax.ShapeDtypeStruct((ROWS, CONV_WIDTH), F32),
            head_major, head_major,
        ],
        scratch_shapes=[
            pltpu.VMEM((2, MIX_TM, PROJ_COLS), F32),
            block, block, block, block,
            pltpu.VMEM((2, BATCH, HEAD_DIM), F32),
        ],
        compiler_params=_params(("arbitrary",)),
        name="mix_proj_lru",
    )(xt, w, w, lx, lx, lx, lx, lx, lx, cw, cb, wg, bg, lam)


def _merge_kernel(w_ref, wprev_ref, wnext_ref, bg_ref, hf_ref, hb_ref, sgl_ref, sgc_ref,
                  scw_ref, scb_ref, wlo_ref, wco_ref, o_ref, z_ref, hcat_ref, res_ref):
    i = pl.program_id(0)
    tm = w_ref.shape[0]

    @pl.when(pl.program_id(1) == 0)
    def _():
        wc = w_ref[...]
        prev = jnp.where(i == 0, 0.0, wprev_ref[...])
        nxt = jnp.where(i == pl.num_programs(0) - 1, 0.0, wnext_ref[...])
        w_dn = jnp.concatenate([prev, wc[:tm - BATCH]], axis=0)
        w_up = jnp.concatenate([wc[BATCH:], nxt], axis=0)
        scw = scw_ref[...]
        v = scw[0:1] * w_dn + scw[1:2] * wc + scw[2:3] * w_up + scb_ref[...]
        z_ref[...] = (bg_ref[...].astype(F32) * v).astype(BF16)
        for h in range(LRU_HEADS):
            hsum = hf_ref[h].astype(F32) + hb_ref[h].astype(F32)
            hcat_ref[:, h * HEAD_DIM:(h + 1) * HEAD_DIM] = hsum.astype(BF16)

    y_lru = jnp.dot(hcat_ref[...], wlo_ref[...], preferred_element_type=F32)
    y_conv = jnp.dot(z_ref[...], wco_ref[...], preferred_element_type=F32)
    merged = sgl_ref[...].astype(F32) * y_lru + sgc_ref[...].astype(F32) * y_conv

    nslab = res_ref.shape[0]
    for k in range(nslab):
        res_ref[k] = merged[:, k * LANES:(k + 1) * LANES]
    for b in range(BATCH):
        for k in range(nslab):
            o_ref[b, :, k * LANES:(k + 1) * LANES] = (
                res_ref[k, pl.ds(b, tm // BATCH, stride=BATCH), :].astype(o_ref.dtype))


def _merge(w, bgate, hf, hb, sg, scw, scb, wlo, wco, tm=512, tn=512):
    nb = tm // BATCH
    last = ROWS // BATCH - 1
    gc_off = D_MODEL // tn
    heads = pl.BlockSpec((LRU_HEADS, tm, HEAD_DIM), lambda i, j: (0, i, 0))
    return pl.pallas_call(
        _merge_kernel,
        grid=(ROWS // tm, D_MODEL // tn),
        in_specs=[
            pl.BlockSpec((tm, CONV_WIDTH), lambda i, j: (i, 0)),
            pl.BlockSpec((BATCH, CONV_WIDTH), lambda i, j: (jnp.maximum(i * nb - 1, 0), 0)),
            pl.BlockSpec((BATCH, CONV_WIDTH), lambda i, j: (jnp.minimum((i + 1) * nb, last), 0)),
            pl.BlockSpec((tm, CONV_WIDTH), lambda i, j: (i, 0)),
            heads, heads,
            pl.BlockSpec((tm, tn), lambda i, j: (i, j)),
            pl.BlockSpec((tm, tn), lambda i, j: (i, j + gc_off)),
            pl.BlockSpec((SC_TAPS, CONV_WIDTH), lambda i, j: (0, 0)),
            pl.BlockSpec((1, CONV_WIDTH), lambda i, j: (0, 0)),
            pl.BlockSpec((LRU_WIDTH, tn), lambda i, j: (0, j)),
            pl.BlockSpec((CONV_WIDTH, tn), lambda i, j: (0, j)),
        ],
        out_specs=pl.BlockSpec((BATCH, tm // BATCH, tn), lambda i, j: (0, i, j)),
        out_shape=jax.ShapeDtypeStruct((BATCH, SEQ, D_MODEL), BF16),
        scratch_shapes=[pltpu.VMEM((tm, CONV_WIDTH), BF16),
                        pltpu.VMEM((tm, LRU_WIDTH), BF16),
                        pltpu.VMEM((tn // LANES, tm, LANES), F32)],
        compiler_params=_params(("parallel", "arbitrary")),
        name="mixer_merge",
    )(w, w, w, bgate, hf, hb, sg, sg, scw, scb, wlo, wco)


def _layer_norm(y, g, b):
    mu = jnp.mean(y, axis=-1, keepdims=True)
    d = y - mu
    var = jnp.mean(d * d, axis=-1, keepdims=True)
    return d * lax.rsqrt(var + LN_EPS) * g + b


OUTPROJ_SUB_ROWS = 256


def _outproj_kernel(m_ref, x_ref, wo_ref, g_ref, b_ref, o_ref):
    sub = OUTPROJ_SUB_ROWS
    for r in range(0, m_ref.shape[0], sub):
        y = jnp.dot(m_ref[r:r + sub, :], wo_ref[...], preferred_element_type=F32)
        o_ref[r:r + sub, :] = _layer_norm(ALPHA * x_ref[r:r + sub, :] + y, g_ref[...], b_ref[...])


def _outproj(merged, x2d, wo, g, b, tm=512):
    return pl.pallas_call(
        _outproj_kernel,
        grid=(ROWS // tm,),
        in_specs=[
            pl.BlockSpec((tm, D_MODEL), lambda i: (i, 0)),
            pl.BlockSpec((tm, D_MODEL), lambda i: (i, 0)),
            pl.BlockSpec((D_MODEL, D_MODEL), lambda i: (0, 0)),
            pl.BlockSpec((1, D_MODEL), lambda i: (0, 0)),
            pl.BlockSpec((1, D_MODEL), lambda i: (0, 0)),
        ],
        out_specs=pl.BlockSpec((tm, D_MODEL), lambda i: (i, 0)),
        out_shape=jax.ShapeDtypeStruct((ROWS, D_MODEL), F32),
        compiler_params=_params(("parallel",)),
        name="out_proj_ln1",
    )(merged, x2d, wo, g, b)


def _mlp_kernel(h_ref, w1_ref, b1_ref, w2_ref, b2_ref, g_ref, b_ref, o_ref, hb_ref):
    f = pl.program_id(1)

    @pl.when(f == 0)
    def _():
        h = h_ref[...]
        hb_ref[...] = h.astype(BF16)
        o_ref[...] = ALPHA * h + b2_ref[...]

    a = jnp.dot(hb_ref[...], w1_ref[...], preferred_element_type=F32) + b1_ref[...]
    a = jnp.maximum(a, 0.0)
    o_ref[...] += jnp.dot((a * a).astype(BF16), w2_ref[...], preferred_element_type=F32)

    @pl.when(f == pl.num_programs(1) - 1)
    def _():
        o_ref[...] = _layer_norm(o_ref[...], g_ref[...], b_ref[...])


def _mlp(h, w1, b1, w2, b2, g, b, tm=512, tf=1024):
    return pl.pallas_call(
        _mlp_kernel,
        grid=(ROWS // tm, D_FF // tf),
        in_specs=[
            pl.BlockSpec((tm, D_MODEL), lambda i, f: (i, 0)),
            pl.BlockSpec((D_MODEL, tf), lambda i, f: (0, f)),
            pl.BlockSpec((1, tf), lambda i, f: (0, f)),
            pl.BlockSpec((tf, D_MODEL), lambda i, f: (f, 0)),
            pl.BlockSpec((1, D_MODEL), lambda i, f: (0, 0)),
            pl.BlockSpec((1, D_MODEL), lambda i, f: (0, 0)),
            pl.BlockSpec((1, D_MODEL), lambda i, f: (0, 0)),
        ],
        out_specs=pl.BlockSpec((tm, D_MODEL), lambda i, f: (i, 0)),
        out_shape=jax.ShapeDtypeStruct((ROWS, D_MODEL), F32),
        scratch_shapes=[pltpu.VMEM((tm, D_MODEL), BF16)],
        compiler_params=_params(("parallel", "arbitrary")),
        name="mlp_ln2",
    )(h, w1, b1, w2, b2, g, b)


def kernel(x, w_in, lru_conv_w, lru_conv_b, lru_w_a, lru_b_a, lru_w_x, lru_b_x, lru_lambda,
           w_lru_out, sc_conv_w, sc_conv_b, w_conv_out, w_o, ln1_g, ln1_b, mlp_w1, mlp_b1,
           mlp_w2, mlp_b2, ln2_g, ln2_b):
    xt = _to_time_major(x)
    w_in_b = w_in.astype(BF16)
    lx = _proj_heads(xt, w_in_b)

    wg = jnp.concatenate([lru_w_a[0], lru_w_x[0], lru_w_a[1], lru_w_x[1]], axis=-1).astype(BF16)
    bg = jnp.concatenate([lru_b_a[0], lru_b_x[0], lru_b_a[1], lru_b_x[1]], axis=-1)
    bg = (0.5 * bg).reshape(LRU_HEADS, 1, 4 * HEAD_DIM)
    lam = lru_lambda.reshape(2, LRU_HEADS, HEAD_DIM).transpose(1, 0, 2)
    sg, cbg, cxc, hf, hb = _mix(xt, w_in_b, lx, 0.5 * lru_conv_w,
                                (0.5 * lru_conv_b).reshape(1, LRU_WIDTH), wg, bg, lam)

    merged = _merge(cxc, cbg, hf, hb, sg, sc_conv_w, sc_conv_b.reshape(1, CONV_WIDTH),
                    w_lru_out.astype(BF16), w_conv_out.astype(BF16))
    h1 = _outproj(merged.reshape(ROWS, D_MODEL), x.reshape(ROWS, D_MODEL), w_o.astype(BF16),
                  ln1_g.reshape(1, D_MODEL), ln1_b.reshape(1, D_MODEL))
    h2 = _mlp(h1, mlp_w1.astype(BF16), mlp_b1.reshape(1, D_FF), mlp_w2.astype(BF16),
              mlp_b2.reshape(1, D_MODEL), ln2_g.reshape(1, D_MODEL), ln2_b.reshape(1, D_MODEL))
    return h2.reshape(BATCH, SEQ, D_MODEL)
```

```python
import functools

import jax
import jax.numpy as jnp
from jax import lax
from jax.experimental import pallas as pl
from jax.experimental.pallas import tpu as pltpu

F32 = jnp.float32
BF16 = jnp.bfloat16

D_MODEL = 2048
BATCH = 8
SEQ = 2048
ROWS = BATCH * SEQ
LRU_WIDTH = 2560
LRU_HEADS = 20
HEAD_DIM = 128
LRU_TAPS = 4
LRU_LEFT = 2
LRU_C = 8.0
CONV_WIDTH = 2048
SC_TAPS = 3
D_FF = 4 * D_MODEL
ALPHA = float(2.0 ** 0.25)
LN_EPS = 1e-5
LANES = 128

OFF_LRU = 0
OFF_CX = LRU_WIDTH
OFF_CB = OFF_CX + CONV_WIDTH
OFF_CC = OFF_CB + CONV_WIDTH
OFF_GL = OFF_CC + CONV_WIDTH
IN_COLS = OFF_GL + 2 * D_MODEL

VMEM_LIMIT = 56 * 1024 * 1024


def _params(sem):
    return pltpu.CompilerParams(dimension_semantics=sem, vmem_limit_bytes=VMEM_LIMIT)


def _to_time_major_kernel(x_ref, o_ref, slab_ref):
    nslab, rows, _ = slab_ref.shape
    t = rows // BATCH
    for g in range(0, D_MODEL // LANES, nslab):
        for k in range(nslab):
            lo = (g + k) * LANES
            for b in range(BATCH):
                slab_ref[k, pl.ds(b, t, stride=BATCH), :] = x_ref[b, :, lo:lo + LANES]
        for k in range(nslab):
            lo = (g + k) * LANES
            o_ref[:, lo:lo + LANES] = slab_ref[k].astype(o_ref.dtype)


def _to_time_major(x, t=128, nslab=4):
    return pl.pallas_call(
        _to_time_major_kernel,
        grid=(SEQ // t,),
        in_specs=[pl.BlockSpec((BATCH, t, D_MODEL), lambda i: (0, i, 0))],
        out_specs=pl.BlockSpec((t * BATCH, D_MODEL), lambda i: (i, 0)),
        out_shape=jax.ShapeDtypeStruct((ROWS, D_MODEL), BF16),
        scratch_shapes=[pltpu.VMEM((nslab, t * BATCH, LANES), F32)],
        compiler_params=_params(("parallel",)),
        name="to_time_major",
    )(x)


PROJ_COLS = 512
BLK_LRU, BLK_CX, BLK_CB, BLK_CC, BLK_GL = (
    off // PROJ_COLS for off in (OFF_LRU, OFF_CX, OFF_CB, OFF_CC, OFF_GL))


def _proj_heads_kernel(x_ref, *refs):
    o_ref = refs[-1]
    nh = PROJ_COLS // HEAD_DIM
    for q, w_ref in enumerate(refs[:-1]):
        acc = jnp.dot(x_ref[...], w_ref[...], preferred_element_type=F32)
        for h in range(nh):
            o_ref[q * nh + h] = acc[:, h * HEAD_DIM:(h + 1) * HEAD_DIM]


def _proj_heads(xt, w, tm=1024):
    nblk = LRU_WIDTH // PROJ_COLS
    w_specs = [
        pl.BlockSpec((D_MODEL, PROJ_COLS), functools.partial(lambda i, q: (0, q), q=BLK_LRU + q))
        for q in range(nblk)
    ]
    return pl.pallas_call(
        _proj_heads_kernel,
        grid=(ROWS // tm,),
        in_specs=[pl.BlockSpec((tm, D_MODEL), lambda i: (i, 0))] + w_specs,
        out_specs=pl.BlockSpec((LRU_HEADS, tm, HEAD_DIM), lambda i: (0, i, 0)),
        out_shape=jax.ShapeDtypeStruct((LRU_HEADS, ROWS, HEAD_DIM), F32),
        compiler_params=_params(("parallel",)),
        name="proj_heads",
    )(xt, *([w] * nblk))


MIX_TM = 1024
MIX_ROW_TILES = ROWS // MIX_TM
LRU_BLOCK_T = 256
LRU_BLOCK_ROWS = LRU_BLOCK_T * BATCH
LRU_NBLOCK = SEQ // LRU_BLOCK_T
LRU_HALO_TOP = 16
LRU_HALO_BOT = 8
MIX_STEPS_G = MIX_ROW_TILES * 4
MIX_STEPS_B = MIX_ROW_TILES * 2
MIX_STEPS_X = MIX_ROW_TILES * 4
MIX_STEPS = MIX_STEPS_G + MIX_STEPS_B + MIX_STEPS_X
assert MIX_STEPS == LRU_HEADS * LRU_NBLOCK


def _mix_section(s):
    in_g = s < MIX_STEPS_G
    in_b = jnp.logical_and(s >= MIX_STEPS_G, s < MIX_STEPS_G + MIX_STEPS_B)
    t = jnp.where(in_g, s, jnp.where(in_b, s - MIX_STEPS_G, s - MIX_STEPS_G - MIX_STEPS_B))
    i = jnp.where(in_b, t // 2, t // 4)
    c = jnp.where(in_b, t % 2, t % 4)
    wa = jnp.where(in_g, BLK_GL + 2 * c, jnp.where(in_b, BLK_CB + 2 * c, BLK_CX + c))
    wb = jnp.where(jnp.logical_or(in_g, in_b), wa + 1, BLK_CC + c)
    return in_g, in_b, i, c, wa, wb


def _softplus(x):
    return jnp.maximum(x, 0.0) + jnp.log1p(jnp.exp(-jnp.abs(x)))


def _mix_kernel(x_ref, wa_ref, wb_ref, lf_ref, lfp_ref, lfn_ref, lb_ref, lbp_ref, lbn_ref,
                cw_ref, cb_ref, wg_ref, bg_ref, lam_ref,
                sg_ref, cbg_ref, cxc_ref, hf_ref, hb_ref,
                raw_ref, u_ref, a_ref, b_ref, st_ref, carry_ref):
    s = pl.program_id(0)
    p = s % LRU_NBLOCK
    n = LRU_BLOCK_ROWS

    @pl.when(p == 0)
    def _():
        carry_ref[...] = jnp.zeros_like(carry_ref)

    x = x_ref[...]
    half = PROJ_COLS // 2

    def proj_slice(q):
        w_ref = wa_ref if q < 2 else wb_ref
        lo = (q % 2) * half
        raw_ref[q // 2, :, lo:lo + half] = jnp.dot(
            x, w_ref[:, lo:lo + half].astype(BF16), preferred_element_type=F32)

    cw = cw_ref[...]
    cb = cb_ref[...]

    def conv(d, main_ref, top_ref, bot_ref, is_first, is_last):
        top = jnp.where(is_first, 0.0, top_ref[...])
        bot = jnp.where(is_last, 0.0, bot_ref[...])
        u = cb + cw[LRU_LEFT:LRU_LEFT + 1] * main_ref[...]
        for k in range(LRU_TAPS):
            off = (k - LRU_LEFT) * BATCH
            if off < 0:
                tap = jnp.concatenate([top[LRU_HALO_TOP + off:], main_ref[pl.ds(0, n + off), :]], axis=0)
            elif off > 0:
                tap = jnp.concatenate([main_ref[pl.ds(off, n - off), :], bot[:off]], axis=0)
            else:
                continue
            u = u + cw[k:k + 1] * tap
        u_ref[d] = u

    half_decay = (-0.5 * LRU_C) * _softplus(-lam_ref[0])
    wg = wg_ref[0]
    bg = bg_ref[0]

    def gate_matmul(d):
        lo = 2 * HEAD_DIM * d
        return jnp.dot(u_ref[d].astype(BF16), wg[:, lo:lo + 2 * HEAD_DIM],
                       preferred_element_type=F32) + bg[:, lo:lo + 2 * HEAD_DIM]

    def gate_math(d, g):
        uh = u_ref[d]
        tr = jnp.tanh(g[:, :HEAD_DIM])
        ti = jnp.tanh(g[:, HEAD_DIM:])
        hd = half_decay[d:d + 1]
        log_a = hd * tr + hd
        a = jnp.exp(log_a)
        m2 = jnp.tanh(log_a) * (-1.0 - a * a)
        m = jnp.where(m2 > 0.0, m2 * lax.rsqrt(m2), 0.0)
        a_ref[d] = a
        b_ref[d] = m * ((ti + 1.0) * uh)

    proj_slice(0)
    conv(0, lf_ref, lfp_ref, lfn_ref, p == 0, p == LRU_NBLOCK - 1)
    conv(1, lb_ref, lbp_ref, lbn_ref, p == LRU_NBLOCK - 1, p == 0)
    g_fwd = gate_matmul(0)
    g_bwd = gate_matmul(1)
    proj_slice(1)
    gate_math(0, g_fwd)
    proj_slice(2)
    gate_math(1, g_bwd)
    proj_slice(3)

    hf = carry_ref[0]
    hb = carry_ref[1]
    for t in range(LRU_BLOCK_T):
        of = t * BATCH
        ob = (LRU_BLOCK_T - 1 - t) * BATCH
        hf = a_ref[0, pl.ds(of, BATCH), :] * hf + b_ref[0, pl.ds(of, BATCH), :]
        hb = a_ref[1, pl.ds(ob, BATCH), :] * hb + b_ref[1, pl.ds(ob, BATCH), :]
        st_ref[0, pl.ds(of, BATCH), :] = hf
        st_ref[1, pl.ds(ob, BATCH), :] = hb
    carry_ref[0] = hf
    carry_ref[1] = hb
    hf_ref[...] = st_ref[0].astype(hf_ref.dtype)
    hb_ref[...] = st_ref[1].astype(hb_ref.dtype)

    @pl.when(s < MIX_STEPS_G)
    def _():
        sg_ref[:, :PROJ_COLS] = (0.5 * jnp.tanh(0.5 * raw_ref[0]) + 0.5).astype(sg_ref.dtype)
        sg_ref[:, PROJ_COLS:] = (0.5 * jnp.tanh(0.5 * raw_ref[1]) + 0.5).astype(sg_ref.dtype)

    @pl.when(jnp.logical_and(s >= MIX_STEPS_G, s < MIX_STEPS_G + MIX_STEPS_B))
    def _():
        cbg_ref[:, :PROJ_COLS] = raw_ref[0].astype(cbg_ref.dtype)
        cbg_ref[:, PROJ_COLS:] = raw_ref[1].astype(cbg_ref.dtype)

    @pl.when(s >= MIX_STEPS_G + MIX_STEPS_B)
    def _():
        cxc_ref[...] = (raw_ref[0] * raw_ref[1]).astype(cxc_ref.dtype)


def _mix(xt, w, lx, cw, cb, wg, bg, lam):
    nb = LRU_NBLOCK
    top_per_block = LRU_BLOCK_ROWS // LRU_HALO_TOP
    bot_per_block = LRU_BLOCK_ROWS // LRU_HALO_BOT
    last_bot = ROWS // LRU_HALO_BOT - 1

    def head(s):
        return s // nb

    def fwd(s):
        return s % nb

    def bwd(s):
        return nb - 1 - s % nb

    def main_spec(blk):
        return pl.BlockSpec((None, LRU_BLOCK_ROWS, HEAD_DIM), lambda s: (head(s), blk(s), 0))

    def top_spec(blk):
        return pl.BlockSpec((None, LRU_HALO_TOP, HEAD_DIM),
                            lambda s: (head(s), jnp.maximum(blk(s) * top_per_block - 1, 0), 0))

    def bot_spec(blk):
        return pl.BlockSpec((None, LRU_HALO_BOT, HEAD_DIM),
                            lambda s: (head(s), jnp.minimum((blk(s) + 1) * bot_per_block, last_bot), 0))

    def out_g(s):
        in_g, _, i, c, _, _ = _mix_section(s)
        return jnp.where(in_g, i, MIX_ROW_TILES - 1), jnp.where(in_g, c, 3)

    def out_b(s):
        in_g, in_b, i, c, _, _ = _mix_section(s)
        return (jnp.where(in_b, i, jnp.where(in_g, 0, MIX_ROW_TILES - 1)),
                jnp.where(in_b, c, jnp.where(in_g, 0, 1)))

    def out_x(s):
        in_g, in_b, i, c, _, _ = _mix_section(s)
        before = jnp.logical_or(in_g, in_b)
        return jnp.where(before, 0, i), jnp.where(before, 0, c)

    block = pltpu.VMEM((2, LRU_BLOCK_ROWS, HEAD_DIM), F32)
    head_major = jax.ShapeDtypeStruct((LRU_HEADS, ROWS, HEAD_DIM), BF16)
    return pl.pallas_call(
        _mix_kernel,
        grid=(MIX_STEPS,),
        in_specs=[
            pl.BlockSpec((MIX_TM, D_MODEL), lambda s: (_mix_section(s)[2], 0)),
            pl.BlockSpec((D_MODEL, PROJ_COLS), lambda s: (0, _mix_section(s)[4])),
            pl.BlockSpec((D_MODEL, PROJ_COLS), lambda s: (0, _mix_section(s)[5])),
            main_spec(fwd), top_spec(fwd), bot_spec(fwd),
            main_spec(bwd), top_spec(bwd), bot_spec(bwd),
            pl.BlockSpec((LRU_TAPS, HEAD_DIM), lambda s: (0, head(s))),
            pl.BlockSpec((1, HEAD_DIM), lambda s: (0, head(s))),
            pl.BlockSpec((1, HEAD_DIM, 4 * HEAD_DIM), lambda s: (head(s), 0, 0)),
            pl.BlockSpec((1, 1, 4 * HEAD_DIM), lambda s: (head(s), 0, 0)),
            pl.BlockSpec((1, 2, HEAD_DIM), lambda s: (head(s), 0, 0)),
        ],
        out_specs=[
            pl.BlockSpec((MIX_TM, 2 * PROJ_COLS), out_g),
            pl.BlockSpec((MIX_TM, 2 * PROJ_COLS), out_b),
            pl.BlockSpec((MIX_TM, PROJ_COLS), out_x),
            pl.BlockSpec((None, LRU_BLOCK_ROWS, HEAD_DIM), lambda s: (head(s), fwd(s), 0)),
            pl.BlockSpec((None, LRU_BLOCK_ROWS, HEAD_DIM), lambda s: (head(s), bwd(s), 0)),
        ],
        out_shape=[
            jax.ShapeDtypeStruct((ROWS, 2 * D_MODEL), BF16),
            jax.ShapeDtypeStruct((ROWS, CONV_WIDTH), BF16),
            jax.ShapeDtypeStruct((ROWS, CONV_WIDTH), BF16),
            head_major, head_major,
        ],
        scratch_shapes=[
            pltpu.VMEM((2, MIX_TM, PROJ_COLS), F32),
            block, block, block, block,
            pltpu.VMEM((2, BATCH, HEAD_DIM), F32),
        ],
        compiler_params=_params(("arbitrary",)),
        name="mix_proj_lru",
    )(xt, w, w, lx, lx, lx, lx, lx, lx, cw, cb, wg, bg, lam)


def _merge_kernel(w_ref, wprev_ref, wnext_ref, bg_ref, hf_ref, hb_ref, sgl_ref, sgc_ref,
                  scw_ref, scb_ref, wlo_ref, wco_ref, o_ref, z_ref, hcat_ref, res_ref):
    i = pl.program_id(0)
    tm = w_ref.shape[0]

    @pl.when(pl.program_id(1) == 0)
    def _():
        wc = w_ref[...].astype(F32)
        prev = jnp.where(i == 0, 0.0, wprev_ref[...].astype(F32)[MERGE_HALO - BATCH:])
        nxt = jnp.where(i == pl.num_programs(0) - 1, 0.0, wnext_ref[...].astype(F32)[:BATCH])
        w_dn = jnp.concatenate([prev, wc[:tm - BATCH]], axis=0)
        w_up = jnp.concatenate([wc[BATCH:], nxt], axis=0)
        scw = scw_ref[...]
        v = scw[0:1] * w_dn + scw[1:2] * wc + scw[2:3] * w_up + scb_ref[...]
        z_ref[...] = (bg_ref[...].astype(F32) * v).astype(BF16)
        for h in range(LRU_HEADS):
            hsum = hf_ref[h].astype(F32) + hb_ref[h].astype(F32)
            hcat_ref[:, h * HEAD_DIM:(h + 1) * HEAD_DIM] = hsum.astype(BF16)

    y_lru = jnp.dot(hcat_ref[...], wlo_ref[...], preferred_element_type=F32)
    y_conv = jnp.dot(z_ref[...], wco_ref[...], preferred_element_type=F32)
    merged = sgl_ref[...].astype(F32) * y_lru + sgc_ref[...].astype(F32) * y_conv

    nslab = res_ref.shape[0]
    for k in range(nslab):
        res_ref[k] = merged[:, k * LANES:(k + 1) * LANES]
    for b in range(BATCH):
        for k in range(nslab):
            o_ref[b, :, k * LANES:(k + 1) * LANES] = (
                res_ref[k, pl.ds(b, tm // BATCH, stride=BATCH), :].astype(o_ref.dtype))


MERGE_HALO = 16


def _merge(w, bgate, hf, hb, sg, scw, scb, wlo, wco, tm=512, tn=1024):
    nb = tm // MERGE_HALO
    last = ROWS // MERGE_HALO - 1
    gc_off = D_MODEL // tn
    heads = pl.BlockSpec((LRU_HEADS, tm, HEAD_DIM), lambda i, j: (0, i, 0))
    return pl.pallas_call(
        _merge_kernel,
        grid=(ROWS // tm, D_MODEL // tn),
        in_specs=[
            pl.BlockSpec((tm, CONV_WIDTH), lambda i, j: (i, 0)),
            pl.BlockSpec((MERGE_HALO, CONV_WIDTH), lambda i, j: (jnp.maximum(i * nb - 1, 0), 0)),
            pl.BlockSpec((MERGE_HALO, CONV_WIDTH), lambda i, j: (jnp.minimum((i + 1) * nb, last), 0)),
            pl.BlockSpec((tm, CONV_WIDTH), lambda i, j: (i, 0)),
            heads, heads,
            pl.BlockSpec((tm, tn), lambda i, j: (i, j)),
            pl.BlockSpec((tm, tn), lambda i, j: (i, j + gc_off)),
            pl.BlockSpec((SC_TAPS, CONV_WIDTH), lambda i, j: (0, 0)),
            pl.BlockSpec((1, CONV_WIDTH), lambda i, j: (0, 0)),
            pl.BlockSpec((LRU_WIDTH, tn), lambda i, j: (0, j)),
            pl.BlockSpec((CONV_WIDTH, tn), lambda i, j: (0, j)),
        ],
        out_specs=pl.BlockSpec((BATCH, tm // BATCH, tn), lambda i, j: (0, i, j)),
        out_shape=jax.ShapeDtypeStruct((BATCH, SEQ, D_MODEL), BF16),
        scratch_shapes=[pltpu.VMEM((tm, CONV_WIDTH), BF16),
                        pltpu.VMEM((tm, LRU_WIDTH), BF16),
                        pltpu.VMEM((tn // LANES, tm, LANES), F32)],
        compiler_params=_params(("parallel", "arbitrary")),
        name="mixer_merge",
    )(w, w, w, bgate, hf, hb, sg, sg, scw, scb, wlo, wco)


def _layer_norm(y, g, b):
    mu = jnp.mean(y, axis=-1, keepdims=True)
    d = y - mu
    var = jnp.mean(d * d, axis=-1, keepdims=True)
    return d * lax.rsqrt(var + LN_EPS) * g + b


OUTPROJ_SUB_ROWS = 256


def _outproj_kernel(m_ref, x_ref, wo_ref, g_ref, b_ref, o_ref):
    sub = OUTPROJ_SUB_ROWS
    for r in range(0, m_ref.shape[0], sub):
        y = jnp.dot(m_ref[r:r + sub, :], wo_ref[...], preferred_element_type=F32)
        o_ref[r:r + sub, :] = _layer_norm(ALPHA * x_ref[r:r + sub, :] + y, g_ref[...], b_ref[...])


def _outproj(merged, x2d, wo, g, b, tm=512):
    return pl.pallas_call(
        _outproj_kernel,
        grid=(ROWS // tm,),
        in_specs=[
            pl.BlockSpec((tm, D_MODEL), lambda i: (i, 0)),
            pl.BlockSpec((tm, D_MODEL), lambda i: (i, 0)),
            pl.BlockSpec((D_MODEL, D_MODEL), lambda i: (0, 0)),
            pl.BlockSpec((1, D_MODEL), lambda i: (0, 0)),
            pl.BlockSpec((1, D_MODEL), lambda i: (0, 0)),
        ],
        out_specs=pl.BlockSpec((tm, D_MODEL), lambda i: (i, 0)),
        out_shape=jax.ShapeDtypeStruct((ROWS, D_MODEL), F32),
        compiler_params=_params(("parallel",)),
        name="out_proj_ln1",
    )(merged, x2d, wo, g, b)


def _mlp_kernel(h_ref, w1_ref, b1_ref, w2_ref, b2_ref, g_ref, b_ref, o_ref, hb_ref):
    f = pl.program_id(1)

    @pl.when(f == 0)
    def _():
        h = h_ref[...]
        hb_ref[...] = h.astype(BF16)
        o_ref[...] = ALPHA * h + b2_ref[...]

    a = jnp.dot(hb_ref[...], w1_ref[...], preferred_element_type=F32) + b1_ref[...]
    a = jnp.maximum(a, 0.0)
    o_ref[...] += jnp.dot((a * a).astype(BF16), w2_ref[...], preferred_element_type=F32)

    @pl.when(f == pl.num_programs(1) - 1)
    def _():
        o_ref[...] = _layer_norm(o_ref[...], g_ref[...], b_ref[...])


def _mlp(h, w1, b1, w2, b2, g, b, tm=512, tf=1024):
    return pl.pallas_call(
        _mlp_kernel,
        grid=(ROWS // tm, D_FF // tf),
        in_specs=[
            pl.BlockSpec((tm, D_MODEL), lambda i, f: (i, 0)),
            pl.BlockSpec((D_MODEL, tf), lambda i, f: (0, f)),
            pl.BlockSpec((1, tf), lambda i, f: (0, f)),
            pl.BlockSpec((tf, D_MODEL), lambda i, f: (f, 0)),
            pl.BlockSpec((1, D_MODEL), lambda i, f: (0, 0)),
            pl.BlockSpec((1, D_MODEL), lambda i, f: (0, 0)),
            pl.BlockSpec((1, D_MODEL), lambda i, f: (0, 0)),
        ],
        out_specs=pl.BlockSpec((tm, D_MODEL), lambda i, f: (i, 0)),
        out_shape=jax.ShapeDtypeStruct((ROWS, D_MODEL), F32),
        scratch_shapes=[pltpu.VMEM((tm, D_MODEL), BF16)],
        compiler_params=_params(("parallel", "arbitrary")),
        name="mlp_ln2",
    )(h, w1, b1, w2, b2, g, b)


def kernel(x, w_in, lru_conv_w, lru_conv_b, lru_w_a, lru_b_a, lru_w_x, lru_b_x, lru_lambda,
           w_lru_out, sc_conv_w, sc_conv_b, w_conv_out, w_o, ln1_g, ln1_b, mlp_w1, mlp_b1,
           mlp_w2, mlp_b2, ln2_g, ln2_b):
    xt = _to_time_major(x)
    lx = _proj_heads(xt, w_in[:, :LRU_WIDTH].astype(BF16))

    wg = jnp.concatenate([lru_w_a[0], lru_w_x[0], lru_w_a[1], lru_w_x[1]], axis=-1).astype(BF16)
    bg = jnp.concatenate([lru_b_a[0], lru_b_x[0], lru_b_a[1], lru_b_x[1]], axis=-1)
    bg = (0.5 * bg).reshape(LRU_HEADS, 1, 4 * HEAD_DIM)
    lam = lru_lambda.reshape(2, LRU_HEADS, HEAD_DIM).transpose(1, 0, 2)
    sg, cbg, cxc, hf, hb = _mix(xt, w_in, lx, 0.5 * lru_conv_w,
                                (0.5 * lru_conv_b).reshape(1, LRU_WIDTH), wg, bg, lam)

    merged = _merge(cxc, cbg, hf, hb, sg, sc_conv_w, sc_conv_b.reshape(1, CONV_WIDTH),
                    w_lru_out.astype(BF16), w_conv_out.astype(BF16))
    h1 = _outproj(merged.reshape(ROWS, D_MODEL), x.reshape(ROWS, D_MODEL), w_o.astype(BF16),
                  ln1_g.reshape(1, D_MODEL), ln1_b.reshape(1, D_MODEL))
    h2 = _mlp(h1, mlp_w1.astype(BF16), mlp_b1.reshape(1, D_FF), mlp_w2.astype(BF16),
              mlp_b2.reshape(1, D_MODEL), ln2_g.reshape(1, D_MODEL), ln2_b.reshape(1, D_MODEL))
    return h2.reshape(BATCH, SEQ, D_MODEL)
```

```python
import functools

import jax
import jax.numpy as jnp
from jax import lax
from jax.experimental import pallas as pl
from jax.experimental.pallas import tpu as pltpu

F32 = jnp.float32
BF16 = jnp.bfloat16

D_MODEL = 2048
BATCH = 8
SEQ = 2048
ROWS = BATCH * SEQ
LRU_WIDTH = 2560
LRU_HEADS = 20
HEAD_DIM = 128
LRU_TAPS = 4
LRU_LEFT = 2
LRU_C = 8.0
CONV_WIDTH = 2048
SC_TAPS = 3
D_FF = 4 * D_MODEL
ALPHA = float(2.0 ** 0.25)
LN_EPS = 1e-5
LANES = 128

OFF_LRU = 0
OFF_CX = LRU_WIDTH
OFF_CB = OFF_CX + CONV_WIDTH
OFF_CC = OFF_CB + CONV_WIDTH
OFF_GL = OFF_CC + CONV_WIDTH
IN_COLS = OFF_GL + 2 * D_MODEL

VMEM_LIMIT = 56 * 1024 * 1024


def _params(sem):
    return pltpu.CompilerParams(dimension_semantics=sem, vmem_limit_bytes=VMEM_LIMIT)


PROJ_COLS = 512
BLK_LRU, BLK_CX, BLK_CB, BLK_CC, BLK_GL = (
    off // PROJ_COLS for off in (OFF_LRU, OFF_CX, OFF_CB, OFF_CC, OFF_GL))
HEADS_KGROUP = 512


def _proj_heads_kernel(x_ref, *refs):
    w_refs, (lx_ref, xt_ref, slab_ref) = refs[:-3], refs[-3:]
    nslab, rows, _ = slab_ref.shape
    t = rows // BATCH
    nh = PROJ_COLS // HEAD_DIM
    accs = [None] * len(w_refs)
    for g in range(D_MODEL // HEADS_KGROUP):
        k0 = g * HEADS_KGROUP
        for k in range(nslab):
            lo = k0 + k * LANES
            for b in range(BATCH):
                slab_ref[k, pl.ds(b, t, stride=BATCH), :] = x_ref[b, :, lo:lo + LANES]
        for k in range(nslab):
            lo = k0 + k * LANES
            xt_ref[:, lo:lo + LANES] = slab_ref[k].astype(xt_ref.dtype)
        xg = xt_ref[:, k0:k0 + HEADS_KGROUP]
        for q, w_ref in enumerate(w_refs):
            part = jnp.dot(xg, w_ref[k0:k0 + HEADS_KGROUP, :], preferred_element_type=F32)
            accs[q] = part if g == 0 else accs[q] + part
    for q, acc in enumerate(accs):
        for h in range(nh):
            lx_ref[q * nh + h] = acc[:, h * HEAD_DIM:(h + 1) * HEAD_DIM]


def _proj_heads(x, w, t=64):
    tm = t * BATCH
    nblk = LRU_WIDTH // PROJ_COLS
    w_specs = [
        pl.BlockSpec((D_MODEL, PROJ_COLS), functools.partial(lambda i, q: (0, q), q=BLK_LRU + q))
        for q in range(nblk)
    ]
    return pl.pallas_call(
        _proj_heads_kernel,
        grid=(SEQ // t,),
        in_specs=[pl.BlockSpec((BATCH, t, D_MODEL), lambda i: (0, i, 0))] + w_specs,
        out_specs=[pl.BlockSpec((LRU_HEADS, tm, HEAD_DIM), lambda i: (0, i, 0)),
                   pl.BlockSpec((tm, D_MODEL), lambda i: (i, 0))],
        out_shape=[jax.ShapeDtypeStruct((LRU_HEADS, ROWS, HEAD_DIM), F32),
                   jax.ShapeDtypeStruct((ROWS, D_MODEL), BF16)],
        scratch_shapes=[pltpu.VMEM((HEADS_KGROUP // LANES, tm, LANES), F32)],
        compiler_params=_params(("parallel",)),
        name="proj_heads",
    )(x, *([w] * nblk))


MIX_TM = 1024
MIX_ROW_TILES = ROWS // MIX_TM
LRU_BLOCK_T = 256
LRU_BLOCK_ROWS = LRU_BLOCK_T * BATCH
LRU_NBLOCK = SEQ // LRU_BLOCK_T
LRU_HALO_TOP = 16
LRU_HALO_BOT = 8
MIX_STEPS_G = MIX_ROW_TILES * 4
MIX_STEPS_B = MIX_ROW_TILES * 2
MIX_STEPS_X = MIX_ROW_TILES * 4
MIX_STEPS = MIX_STEPS_G + MIX_STEPS_B + MIX_STEPS_X
assert MIX_STEPS == LRU_HEADS * LRU_NBLOCK


def _mix_section(s):
    in_g = s < MIX_STEPS_G
    in_b = jnp.logical_and(s >= MIX_STEPS_G, s < MIX_STEPS_G + MIX_STEPS_B)
    t = jnp.where(in_g, s, jnp.where(in_b, s - MIX_STEPS_G, s - MIX_STEPS_G - MIX_STEPS_B))
    i = jnp.where(in_b, t // 2, t // 4)
    c = jnp.where(in_b, t % 2, t % 4)
    wa = jnp.where(in_g, BLK_GL + 2 * c, jnp.where(in_b, BLK_CB + 2 * c, BLK_CX + c))
    wb = jnp.where(jnp.logical_or(in_g, in_b), wa + 1, BLK_CC + c)
    return in_g, in_b, i, c, wa, wb


def _softplus(x):
    return jnp.maximum(x, 0.0) + jnp.log1p(jnp.exp(-jnp.abs(x)))


def _mix_kernel(x_ref, wa_ref, wb_ref, lf_ref, lfp_ref, lfn_ref, lb_ref, lbp_ref, lbn_ref,
                cw_ref, cb_ref, wg_ref, bg_ref, lam_ref,
                sg_ref, cbg_ref, cxc_ref, hf_ref, hb_ref,
                raw_ref, u_ref, a_ref, b_ref, st_ref, carry_ref):
    s = pl.program_id(0)
    p = s % LRU_NBLOCK
    n = LRU_BLOCK_ROWS

    @pl.when(p == 0)
    def _():
        carry_ref[...] = jnp.zeros_like(carry_ref)

    x = x_ref[...]
    half = PROJ_COLS // 2

    def proj_slice(q):
        w_ref = wa_ref if q < 2 else wb_ref
        lo = (q % 2) * half
        raw_ref[q // 2, :, lo:lo + half] = jnp.dot(
            x, w_ref[:, lo:lo + half].astype(BF16), preferred_element_type=F32)

    cw = cw_ref[...]
    cb = cb_ref[...]

    def conv(d, main_ref, top_ref, bot_ref, is_first, is_last):
        top = jnp.where(is_first, 0.0, top_ref[...])
        bot = jnp.where(is_last, 0.0, bot_ref[...])
        u = cb + cw[LRU_LEFT:LRU_LEFT + 1] * main_ref[...]
        for k in range(LRU_TAPS):
            off = (k - LRU_LEFT) * BATCH
            if off < 0:
                tap = jnp.concatenate([top[LRU_HALO_TOP + off:], main_ref[pl.ds(0, n + off), :]], axis=0)
            elif off > 0:
                tap = jnp.concatenate([main_ref[pl.ds(off, n - off), :], bot[:off]], axis=0)
            else:
                continue
            u = u + cw[k:k + 1] * tap
        u_ref[d] = u

    half_decay = (-0.5 * LRU_C) * _softplus(-lam_ref[0])
    wg = wg_ref[0]
    bg = bg_ref[0]

    def gate_matmul(d):
        lo = 2 * HEAD_DIM * d
        return jnp.dot(u_ref[d].astype(BF16), wg[:, lo:lo + 2 * HEAD_DIM],
                       preferred_element_type=F32) + bg[:, lo:lo + 2 * HEAD_DIM]

    def gate_math(d, g):
        uh = u_ref[d]
        tr = jnp.tanh(g[:, :HEAD_DIM])
        ti = jnp.tanh(g[:, HEAD_DIM:])
        hd = half_decay[d:d + 1]
        log_a = hd * tr + hd
        a = jnp.exp(log_a)
        m2 = jnp.tanh(log_a) * (-1.0 - a * a)
        m = jnp.where(m2 > 0.0, m2 * lax.rsqrt(m2), 0.0)
        a_ref[d] = a
        b_ref[d] = m * ((ti + 1.0) * uh)

    proj_slice(0)
    conv(0, lf_ref, lfp_ref, lfn_ref, p == 0, p == LRU_NBLOCK - 1)
    conv(1, lb_ref, lbp_ref, lbn_ref, p == LRU_NBLOCK - 1, p == 0)
    g_fwd = gate_matmul(0)
    g_bwd = gate_matmul(1)
    proj_slice(1)
    gate_math(0, g_fwd)
    proj_slice(2)
    gate_math(1, g_bwd)
    proj_slice(3)

    hf = carry_ref[0]
    hb = carry_ref[1]
    for t in range(LRU_BLOCK_T):
        of = t * BATCH
        ob = (LRU_BLOCK_T - 1 - t) * BATCH
        hf = a_ref[0, pl.ds(of, BATCH), :] * hf + b_ref[0, pl.ds(of, BATCH), :]
        hb = a_ref[1, pl.ds(ob, BATCH), :] * hb + b_ref[1, pl.ds(ob, BATCH), :]
        st_ref[0, pl.ds(of, BATCH), :] = hf
        st_ref[1, pl.ds(ob, BATCH), :] = hb
    carry_ref[0] = hf
    carry_ref[1] = hb
    hf_ref[...] = st_ref[0].astype(hf_ref.dtype)
    hb_ref[...] = st_ref[1].astype(hb_ref.dtype)

    @pl.when(s < MIX_STEPS_G)
    def _():
        sg_ref[:, :PROJ_COLS] = (0.5 * jnp.tanh(0.5 * raw_ref[0]) + 0.5).astype(sg_ref.dtype)
        sg_ref[:, PROJ_COLS:] = (0.5 * jnp.tanh(0.5 * raw_ref[1]) + 0.5).astype(sg_ref.dtype)

    @pl.when(jnp.logical_and(s >= MIX_STEPS_G, s < MIX_STEPS_G + MIX_STEPS_B))
    def _():
        cbg_ref[:, :PROJ_COLS] = raw_ref[0].astype(cbg_ref.dtype)
        cbg_ref[:, PROJ_COLS:] = raw_ref[1].astype(cbg_ref.dtype)

    @pl.when(s >= MIX_STEPS_G + MIX_STEPS_B)
    def _():
        cxc_ref[...] = (raw_ref[0] * raw_ref[1]).astype(cxc_ref.dtype)


def _mix(xt, w, lx, cw, cb, wg, bg, lam):
    nb = LRU_NBLOCK
    top_per_block = LRU_BLOCK_ROWS // LRU_HALO_TOP
    bot_per_block = LRU_BLOCK_ROWS // LRU_HALO_BOT
    last_bot = ROWS // LRU_HALO_BOT - 1

    def head(s):
        return s // nb

    def fwd(s):
        return s % nb

    def bwd(s):
        return nb - 1 - s % nb

    def main_spec(blk):
        return pl.BlockSpec((None, LRU_BLOCK_ROWS, HEAD_DIM), lambda s: (head(s), blk(s), 0))

    def top_spec(blk):
        return pl.BlockSpec((None, LRU_HALO_TOP, HEAD_DIM),
                            lambda s: (head(s), jnp.maximum(blk(s) * top_per_block - 1, 0), 0))

    def bot_spec(blk):
        return pl.BlockSpec((None, LRU_HALO_BOT, HEAD_DIM),
                            lambda s: (head(s), jnp.minimum((blk(s) + 1) * bot_per_block, last_bot), 0))

    def out_g(s):
        in_g, _, i, c, _, _ = _mix_section(s)
        return jnp.where(in_g, i, MIX_ROW_TILES - 1), jnp.where(in_g, c, 3)

    def out_b(s):
        in_g, in_b, i, c, _, _ = _mix_section(s)
        return (jnp.where(in_b, i, jnp.where(in_g, 0, MIX_ROW_TILES - 1)),
                jnp.where(in_b, c, jnp.where(in_g, 0, 1)))

    def out_x(s):
        in_g, in_b, i, c, _, _ = _mix_section(s)
        before = jnp.logical_or(in_g, in_b)
        return jnp.where(before, 0, i), jnp.where(before, 0, c)

    block = pltpu.VMEM((2, LRU_BLOCK_ROWS, HEAD_DIM), F32)
    head_major = jax.ShapeDtypeStruct((LRU_HEADS, ROWS, HEAD_DIM), BF16)
    return pl.pallas_call(
        _mix_kernel,
        grid=(MIX_STEPS,),
        in_specs=[
            pl.BlockSpec((MIX_TM, D_MODEL), lambda s: (_mix_section(s)[2], 0)),
            pl.BlockSpec((D_MODEL, PROJ_COLS), lambda s: (0, _mix_section(s)[4])),
            pl.BlockSpec((D_MODEL, PROJ_COLS), lambda s: (0, _mix_section(s)[5])),
            main_spec(fwd), top_spec(fwd), bot_spec(fwd),
            main_spec(bwd), top_spec(bwd), bot_spec(bwd),
            pl.BlockSpec((LRU_TAPS, HEAD_DIM), lambda s: (0, head(s))),
            pl.BlockSpec((1, HEAD_DIM), lambda s: (0, head(s))),
            pl.BlockSpec((1, HEAD_DIM, 4 * HEAD_DIM), lambda s: (head(s), 0, 0)),
            pl.BlockSpec((1, 1, 4 * HEAD_DIM), lambda s: (head(s), 0, 0)),
            pl.BlockSpec((1, 2, HEAD_DIM), lambda s: (head(s), 0, 0)),
        ],
        out_specs=[
            pl.BlockSpec((MIX_TM, 2 * PROJ_COLS), out_g),
            pl.BlockSpec((MIX_TM, 2 * PROJ_COLS), out_b),
            pl.BlockSpec((MIX_TM, PROJ_COLS), out_x),
            pl.BlockSpec((None, LRU_BLOCK_ROWS, HEAD_DIM), lambda s: (head(s), fwd(s), 0)),
            pl.BlockSpec((None, LRU_BLOCK_ROWS, HEAD_DIM), lambda s: (head(s), bwd(s), 0)),
        ],
        out_shape=[
            jax.ShapeDtypeStruct((ROWS, 2 * D_MODEL), BF16),
            jax.ShapeDtypeStruct((ROWS, CONV_WIDTH), BF16),
            jax.ShapeDtypeStruct((ROWS, CONV_WIDTH), BF16),
            head_major, head_major,
        ],
        scratch_shapes=[
            pltpu.VMEM((2, MIX_TM, PROJ_COLS), F32),
            block, block, block, block,
            pltpu.VMEM((2, BATCH, HEAD_DIM), F32),
        ],
        compiler_params=_params(("arbitrary",)),
        name="mix_proj_lru",
    )(xt, w, w, lx, lx, lx, lx, lx, lx, cw, cb, wg, bg, lam)


MERGE_HALO = 16


def _merge_kernel(w_ref, wprev_ref, wnext_ref, bg_ref, hf_ref, hb_ref, sgl_ref, sgc_ref,
                  scw_ref, scb_ref, wlo_ref, wco_ref, o_ref, z_ref, hcat_ref, res_ref):
    i = pl.program_id(0)
    tm = w_ref.shape[0]

    @pl.when(pl.program_id(1) == 0)
    def _():
        wc = w_ref[...].astype(F32)
        prev = jnp.where(i == 0, 0.0, wprev_ref[...].astype(F32)[MERGE_HALO - BATCH:])
        nxt = jnp.where(i == pl.num_programs(0) - 1, 0.0, wnext_ref[...].astype(F32)[:BATCH])
        w_dn = jnp.concatenate([prev, wc[:tm - BATCH]], axis=0)
        w_up = jnp.concatenate([wc[BATCH:], nxt], axis=0)
        scw = scw_ref[...]
        v = scw[0:1] * w_dn + scw[1:2] * wc + scw[2:3] * w_up + scb_ref[...]
        z_ref[...] = (bg_ref[...].astype(F32) * v).astype(BF16)
        for h in range(LRU_HEADS):
            hsum = hf_ref[h].astype(F32) + hb_ref[h].astype(F32)
            hcat_ref[:, h * HEAD_DIM:(h + 1) * HEAD_DIM] = hsum.astype(BF16)

    y_lru = jnp.dot(hcat_ref[...], wlo_ref[...], preferred_element_type=F32)
    y_conv = jnp.dot(z_ref[...], wco_ref[...], preferred_element_type=F32)
    merged = sgl_ref[...].astype(F32) * y_lru + sgc_ref[...].astype(F32) * y_conv

    nslab = res_ref.shape[0]
    for k in range(nslab):
        res_ref[k] = merged[:, k * LANES:(k + 1) * LANES]
    for b in range(BATCH):
        for k in range(nslab):
            o_ref[b, :, k * LANES:(k + 1) * LANES] = (
                res_ref[k, pl.ds(b, tm // BATCH, stride=BATCH), :].astype(o_ref.dtype))


def _merge(w, bgate, hf, hb, sg, scw, scb, wlo, wco, tm=512, tn=1024):
    nb = tm // MERGE_HALO
    last = ROWS // MERGE_HALO - 1
    gc_off = D_MODEL // tn
    heads = pl.BlockSpec((LRU_HEADS, tm, HEAD_DIM), lambda i, j: (0, i, 0))
    return pl.pallas_call(
        _merge_kernel,
        grid=(ROWS // tm, D_MODEL // tn),
        in_specs=[
            pl.BlockSpec((tm, CONV_WIDTH), lambda i, j: (i, 0)),
            pl.BlockSpec((MERGE_HALO, CONV_WIDTH), lambda i, j: (jnp.maximum(i * nb - 1, 0), 0)),
            pl.BlockSpec((MERGE_HALO, CONV_WIDTH), lambda i, j: (jnp.minimum((i + 1) * nb, last), 0)),
            pl.BlockSpec((tm, CONV_WIDTH), lambda i, j: (i, 0)),
            heads, heads,
            pl.BlockSpec((tm, tn), lambda i, j: (i, j)),
            pl.BlockSpec((tm, tn), lambda i, j: (i, j + gc_off)),
            pl.BlockSpec((SC_TAPS, CONV_WIDTH), lambda i, j: (0, 0)),
            pl.BlockSpec((1, CONV_WIDTH), lambda i, j: (0, 0)),
            pl.BlockSpec((LRU_WIDTH, tn), lambda i, j: (0, j)),
            pl.BlockSpec((CONV_WIDTH, tn), lambda i, j: (0, j)),
        ],
        out_specs=pl.BlockSpec((BATCH, tm // BATCH, tn), lambda i, j: (0, i, j)),
        out_shape=jax.ShapeDtypeStruct((BATCH, SEQ, D_MODEL), BF16),
        scratch_shapes=[pltpu.VMEM((tm, CONV_WIDTH), BF16),
                        pltpu.VMEM((tm, LRU_WIDTH), BF16),
                        pltpu.VMEM((tn // LANES, tm, LANES), F32)],
        compiler_params=_params(("parallel", "arbitrary")),
        name="mixer_merge",
    )(w, w, w, bgate, hf, hb, sg, sg, scw, scb, wlo, wco)


def _layer_norm(y, g, b):
    mu = jnp.mean(y, axis=-1, keepdims=True)
    d = y - mu
    var = jnp.mean(d * d, axis=-1, keepdims=True)
    return d * lax.rsqrt(var + LN_EPS) * g + b


OUTPROJ_SUB_ROWS = 256


def _outproj_kernel(m_ref, x_ref, wo_ref, g_ref, b_ref, o_ref):
    sub = OUTPROJ_SUB_ROWS
    for r in range(0, m_ref.shape[0], sub):
        y = jnp.dot(m_ref[r:r + sub, :], wo_ref[...], preferred_element_type=F32)
        o_ref[r:r + sub, :] = _layer_norm(ALPHA * x_ref[r:r + sub, :] + y, g_ref[...], b_ref[...])


def _outproj(merged, x2d, wo, g, b, tm=512):
    return pl.pallas_call(
        _outproj_kernel,
        grid=(ROWS // tm,),
        in_specs=[
            pl.BlockSpec((tm, D_MODEL), lambda i: (i, 0)),
            pl.BlockSpec((tm, D_MODEL), lambda i: (i, 0)),
            pl.BlockSpec((D_MODEL, D_MODEL), lambda i: (0, 0)),
            pl.BlockSpec((1, D_MODEL), lambda i: (0, 0)),
            pl.BlockSpec((1, D_MODEL), lambda i: (0, 0)),
        ],
        out_specs=pl.BlockSpec((tm, D_MODEL), lambda i: (i, 0)),
        out_shape=jax.ShapeDtypeStruct((ROWS, D_MODEL), F32),
        compiler_params=_params(("parallel",)),
        name="out_proj_ln1",
    )(merged, x2d, wo, g, b)


def _mlp_kernel(h_ref, w1_ref, b1_ref, w2_ref, b2_ref, g_ref, b_ref, o_ref, hb_ref):
    f = pl.program_id(1)

    @pl.when(f == 0)
    def _():
        h = h_ref[...]
        hb_ref[...] = h.astype(BF16)
        o_ref[...] = ALPHA * h + b2_ref[...]

    a = jnp.dot(hb_ref[...], w1_ref[...], preferred_element_type=F32) + b1_ref[...]
    a = jnp.maximum(a, 0.0)
    o_ref[...] += jnp.dot((a * a).astype(BF16), w2_ref[...], preferred_element_type=F32)

    @pl.when(f == pl.num_programs(1) - 1)
    def _():
        o_ref[...] = _layer_norm(o_ref[...], g_ref[...], b_ref[...])


def _mlp(h, w1, b1, w2, b2, g, b, tm=512, tf=1024):
    return pl.pallas_call(
        _mlp_kernel,
        grid=(ROWS // tm, D_FF // tf),
        in_specs=[
            pl.BlockSpec((tm, D_MODEL), lambda i, f: (i, 0)),
            pl.BlockSpec((D_MODEL, tf), lambda i, f: (0, f)),
            pl.BlockSpec((1, tf), lambda i, f: (0, f)),
            pl.BlockSpec((tf, D_MODEL), lambda i, f: (f, 0)),
            pl.BlockSpec((1, D_MODEL), lambda i, f: (0, 0)),
            pl.BlockSpec((1, D_MODEL), lambda i, f: (0, 0)),
            pl.BlockSpec((1, D_MODEL), lambda i, f: (0, 0)),
        ],
        out_specs=pl.BlockSpec((tm, D_MODEL), lambda i, f: (i, 0)),
        out_shape=jax.ShapeDtypeStruct((ROWS, D_MODEL), F32),
        scratch_shapes=[pltpu.VMEM((tm, D_MODEL), BF16)],
        compiler_params=_params(("parallel", "arbitrary")),
        name="mlp_ln2",
    )(h, w1, b1, w2, b2, g, b)


def kernel(x, w_in, lru_conv_w, lru_conv_b, lru_w_a, lru_b_a, lru_w_x, lru_b_x, lru_lambda,
           w_lru_out, sc_conv_w, sc_conv_b, w_conv_out, w_o, ln1_g, ln1_b, mlp_w1, mlp_b1,
           mlp_w2, mlp_b2, ln2_g, ln2_b):
    lx, xt = _proj_heads(x, w_in[:, :LRU_WIDTH].astype(BF16))

    wg = jnp.concatenate([lru_w_a[0], lru_w_x[0], lru_w_a[1], lru_w_x[1]], axis=-1).astype(BF16)
    bg = jnp.concatenate([lru_b_a[0], lru_b_x[0], lru_b_a[1], lru_b_x[1]], axis=-1)
    bg = (0.5 * bg).reshape(LRU_HEADS, 1, 4 * HEAD_DIM)
    lam = lru_lambda.reshape(2, LRU_HEADS, HEAD_DIM).transpose(1, 0, 2)
    sg, cbg, cxc, hf, hb = _mix(xt, w_in, lx, 0.5 * lru_conv_w,
                                (0.5 * lru_conv_b).reshape(1, LRU_WIDTH), wg, bg, lam)

    merged = _merge(cxc, cbg, hf, hb, sg, sc_conv_w, sc_conv_b.reshape(1, CONV_WIDTH),
                    w_lru_out.astype(BF16), w_conv_out.astype(BF16))
    h1 = _outproj(merged.reshape(ROWS, D_MODEL), x.reshape(ROWS, D_MODEL), w_o.astype(BF16),
                  ln1_g.reshape(1, D_MODEL), ln1_b.reshape(1, D_MODEL))
    h2 = _mlp(h1, mlp_w1.astype(BF16), mlp_b1.reshape(1, D_FF), mlp_w2.astype(BF16),
              mlp_b2.reshape(1, D_MODEL), ln2_g.reshape(1, D_MODEL), ln2_b.reshape(1, D_MODEL))
    return h2.reshape(BATCH, SEQ, D_MODEL)
```
